```python
import math
import jax, jax.numpy as jnp
from jax import lax
import numpy as np

D_MODEL = 1024
BATCH = 16
SEQ = 2048
DEPTH = 2

MIX_WIDTH = D_MODEL
DN_HEAD_DIM = 128
DN_WIDTH = D_MODEL // 2
DN_HEADS = DN_WIDTH // DN_HEAD_DIM
SB_HEAD_DIM = 64
SB_WIDTH = MIX_WIDTH - DN_WIDTH
SB_HEADS = SB_WIDTH // SB_HEAD_DIM
CONV_K = 4
CHUNK = 64
SB_BLOCK = 128
D_FF = 2816
N_EXPERTS = 8
TOP_K = 2
D_FF_EXPERT = 3584
N_DENSE = (DEPTH + 1) // 2
N_MOE = DEPTH // 2
EPS = 1e-6
IN_SPLITS = (DN_WIDTH, DN_WIDTH, DN_WIDTH, DN_WIDTH, DN_HEADS, DN_HEADS, SB_WIDTH, SB_WIDTH, SB_WIDTH)
P_IN = sum(IN_SPLITS)

kernel_name = "hybrid_deltanet_stickbreaking_moe"


def rms_norm(x, w):
    xf = x.astype(jnp.float32)
    y = xf * lax.rsqrt(jnp.mean(xf * xf, axis=-1, keepdims=True) + EPS)
    return (y * w.astype(jnp.float32)).astype(x.dtype)


def l2_normalize(x):
    return x * lax.rsqrt(jnp.sum(x * x, axis=-1, keepdims=True) + EPS)


def causal_depthwise_conv(x, w):
    return lax.conv_general_dilated(
        x, w[:, None, :].astype(x.dtype), window_strides=(1,), padding=((CONV_K - 1, 0),),
        dimension_numbers=("NWC", "WIO", "NWC"), feature_group_count=x.shape[-1])


def chunk_gated_delta_rule(q, k, v, beta, g):
    B, T, H, dk = q.shape
    dv = v.shape[-1]
    N = T // CHUNK
    q = l2_normalize(q) * (dk ** -0.5)
    k = l2_normalize(k)

    def to_chunks(a):
        return a.reshape(B, N, CHUNK, H, a.shape[-1]).transpose(0, 3, 1, 2, 4)

    q, k, v = to_chunks(q), to_chunks(k), to_chunks(v)
    beta = beta.reshape(B, N, CHUNK, H).transpose(0, 3, 1, 2)
    g = jnp.cumsum(g.reshape(B, N, CHUNK, H).transpose(0, 3, 1, 2), axis=-1)

    tril = jnp.tril(jnp.ones((CHUNK, CHUNK), dtype=bool))
    strict = jnp.tril(jnp.ones((CHUNK, CHUNK), dtype=bool), -1)
    decay = jnp.exp(jnp.where(tril, g[..., :, None] - g[..., None, :], -jnp.inf))

    k_beta = k * beta[..., None]
    v_beta = v * beta[..., None]
    L = jnp.where(strict, jnp.einsum("bhncd,bhnsd->bhncs", k_beta, k) * decay, 0.0)
    eye = jnp.eye(CHUNK, dtype=q.dtype)
    rhs = jnp.concatenate([v_beta, k_beta * jnp.exp(g)[..., None]], axis=-1)
    sol = lax.linalg.triangular_solve(L + eye, rhs, left_side=True, lower=True, unit_diagonal=True)
    u, w = sol[..., :dv], sol[..., dv:]

    attn = jnp.einsum("bhncd,bhnsd->bhncs", q, k) * decay
    g_last = g[..., -1]
    q_decay = q * jnp.exp(g)[..., None]
    k_to_end = k * jnp.exp(g_last[..., None] - g)[..., None]

    def step(S, inp):
        qd, kd, u_i, w_i, a_i, gl = inp
        v_new = u_i - jnp.einsum("bhcd,bhde->bhce", w_i, S)
        o = jnp.einsum("bhcd,bhde->bhce", qd, S) + jnp.einsum("bhcs,bhse->bhce", a_i, v_new)
        S = S * jnp.exp(gl)[..., None, None] + jnp.einsum("bhcd,bhce->bhde", kd, v_new)
        return S, o

    xs = tuple(jnp.moveaxis(a, 2, 0) for a in (q_decay, k_to_end, u, w, attn, g_last))
    S0 = jnp.zeros((B, H, dk, dv), dtype=q.dtype)
    _, o = lax.scan(step, S0, xs)
    return o.transpose(1, 0, 3, 2, 4).reshape(B, T, H, dv)


def gated_deltanet(dq, dk_, dv_, dz, db, da, conv_w, a_log, dt_bias, norm_w):
    B, T, _ = dq.shape
    qkv = jax.nn.silu(causal_depthwise_conv(jnp.concatenate([dq, dk_, dv_], axis=-1), conv_w))
    qkv = qkv.astype(jnp.float32)
    q, k, v = jnp.split(qkv, 3, axis=-1)
    q = q.reshape(B, T, DN_HEADS, DN_HEAD_DIM)
    k = k.reshape(B, T, DN_HEADS, DN_HEAD_DIM)
    v = v.reshape(B, T, DN_HEADS, DN_HEAD_DIM)
    beta = jax.nn.sigmoid(db.astype(jnp.float32))
    g = -jnp.exp(a_log.astype(jnp.float32)) * jax.nn.softplus(
        da.astype(jnp.float32) + dt_bias.astype(jnp.float32))
    o = chunk_gated_delta_rule(q, k, v, beta, g)
    z = dz.astype(jnp.float32).reshape(B, T, DN_HEADS, DN_HEAD_DIM)
    o = rms_norm(o, norm_w) * jax.nn.silu(z)
    return o.reshape(B, T, DN_WIDTH).astype(dq.dtype)


def stick_breaking_attention(q, k, v):
    B, T, H, d = q.shape
    nb = T // SB_BLOCK
    qb = q.astype(jnp.float32).reshape(B, nb, SB_BLOCK, H, d).transpose(1, 0, 3, 2, 4)
    kh = k.astype(jnp.float32).transpose(0, 2, 1, 3)
    vh = v.astype(jnp.float32).transpose(0, 2, 1, 3)
    k_pos = jnp.arange(T, dtype=jnp.int32)
    scale = d ** -0.5

    def block(args):
        q_blk, start = args
        z = jnp.einsum("bhqd,bhkd->bhqk", q_blk, kh) * scale
        q_pos = start + jnp.arange(SB_BLOCK, dtype=jnp.int32)
        mask = k_pos[None, :] < q_pos[:, None]
        log_keep = jnp.where(mask, jax.nn.log_sigmoid(-z), 0.0)
        between = lax.cumsum(log_keep, axis=3, reverse=True) - log_keep
        weight = jnp.where(mask, jnp.exp(jax.nn.log_sigmoid(z) + between), 0.0)
        return jnp.einsum("bhqk,bhkd->bhqd", weight, vh)

    starts = jnp.arange(nb, dtype=jnp.int32) * SB_BLOCK
    o = lax.map(block, (qb, starts))
    return o.transpose(1, 0, 3, 2, 4).reshape(B, T, H, d)


def hybrid_mixer(h, w_in, conv_w, a_log, dt_bias, dn_norm_w, sb_norm_w, w_out):
    B, T, _ = h.shape
    proj = jnp.einsum("btd,dp->btp", h, w_in)
    split_points = [int(s) for s in np.cumsum(IN_SPLITS)[:-1]]
    dq, dk_, dv_, dz, db, da, sq, sk, sv = jnp.split(proj, split_points, axis=-1)
    dn_out = gated_deltanet(dq, dk_, dv_, dz, db, da, conv_w, a_log, dt_bias, dn_norm_w)
    sb = stick_breaking_attention(sq.reshape(B, T, SB_HEADS, SB_HEAD_DIM),
                                  sk.reshape(B, T, SB_HEADS, SB_HEAD_DIM),
                                  sv.reshape(B, T, SB_HEADS, SB_HEAD_DIM))
    sb_out = rms_norm(sb, sb_norm_w).reshape(B, T, SB_WIDTH).astype(h.dtype)
    mixed = jnp.concatenate([dn_out, sb_out], axis=-1)
    return jnp.einsum("btm,md->btd", mixed, w_out)


def swiglu(x, w_gate, w_up, w_down):
    return (jax.nn.silu(x @ w_gate) * (x @ w_up)) @ w_down


def moe_swiglu(h, router_w, w_gate, w_up, w_down):
    B, T, D = h.shape
    xt = h.reshape(B * T, D)
    logits = (xt @ router_w).astype(jnp.float32)
    top_val, top_idx = lax.top_k(logits, TOP_K)
    gates = jax.nn.softmax(top_val, axis=-1)
    combine = jnp.einsum("nk,nke->ne", gates, jax.nn.one_hot(top_idx, N_EXPERTS, dtype=jnp.float32))
    y = jnp.zeros((B * T, D), dtype=jnp.float32)
    for e in range(N_EXPERTS):
        y = y + combine[:, e:e + 1] * swiglu(xt, w_gate[e], w_up[e], w_down[e]).astype(jnp.float32)
    return y.reshape(B, T, D).astype(h.dtype)


def setup_inputs(seed: int = 0) -> dict:
    key = jax.random.key(seed)
    ks = jax.random.split(key, 19)

    def normal(k, shape, scale):
        return jax.random.normal(k, shape, jnp.float32) * scale

    def gain(k, shape):
        return 1.0 + 0.05 * jax.random.normal(k, shape, jnp.float32)

    x = normal(ks[0], (BATCH, SEQ, D_MODEL), 1.0)
    norm_mix_pre = gain(ks[1], (DEPTH, D_MODEL))
    norm_mix_post = gain(ks[2], (DEPTH, D_MODEL))
    norm_ffn_pre = gain(ks[3], (DEPTH, D_MODEL))
    norm_ffn_post = gain(ks[4], (DEPTH, D_MODEL))
    w_in = normal(ks[5], (DEPTH, D_MODEL, P_IN), D_MODEL ** -0.5)
    conv_w = normal(ks[6], (DEPTH, CONV_K, 3 * DN_WIDTH), CONV_K ** -0.5)
    dn_a_log = jnp.log(jax.random.uniform(ks[7], (DEPTH, DN_HEADS), jnp.float32, 1.0, 16.0))
    dt = jnp.exp(jax.random.uniform(ks[8], (DEPTH, DN_HEADS), jnp.float32, math.log(1e-3), math.log(1e-1)))
    dn_dt_bias = dt + jnp.log(-jnp.expm1(-dt))
    dn_norm_w = gain(ks[9], (DEPTH, DN_HEAD_DIM))
    sb_norm_w = gain(ks[10], (DEPTH, SB_HEAD_DIM))
    w_out = normal(ks[11], (DEPTH, MIX_WIDTH, D_MODEL), MIX_WIDTH ** -0.5)
    ffn_w_gate = normal(ks[12], (N_DENSE, D_MODEL, D_FF), D_MODEL ** -0.5)
    ffn_w_up = normal(ks[13], (N_DENSE, D_MODEL, D_FF), D_MODEL ** -0.5)
    ffn_w_down = normal(ks[14], (N_DENSE, D_FF, D_MODEL), D_FF ** -0.5)
    router_w = normal(ks[15], (N_MOE, D_MODEL, N_EXPERTS), D_MODEL ** -0.5)
    moe_w_gate = normal(ks[16], (N_MOE, N_EXPERTS, D_MODEL, D_FF_EXPERT), D_MODEL ** -0.5)
    moe_w_up = normal(ks[17], (N_MOE, N_EXPERTS, D_MODEL, D_FF_EXPERT), D_MODEL ** -0.5)
    moe_w_down = normal(ks[18], (N_MOE, N_EXPERTS, D_FF_EXPERT, D_MODEL), D_FF_EXPERT ** -0.5)
    return {"x": x, "norm_mix_pre": norm_mix_pre, "norm_mix_post": norm_mix_post,
            "norm_ffn_pre": norm_ffn_pre, "norm_ffn_post": norm_ffn_post, "w_in": w_in,
            "conv_w": conv_w, "dn_a_log": dn_a_log, "dn_dt_bias": dn_dt_bias, "dn_norm_w": dn_norm_w,
            "sb_norm_w": sb_norm_w, "w_out": w_out, "ffn_w_gate": ffn_w_gate, "ffn_w_up": ffn_w_up,
            "ffn_w_down": ffn_w_down, "router_w": router_w, "moe_w_gate": moe_w_gate,
            "moe_w_up": moe_w_up, "moe_w_down": moe_w_down}


def reference(x, norm_mix_pre, norm_mix_post, norm_ffn_pre, norm_ffn_post, w_in, conv_w,
              dn_a_log, dn_dt_bias, dn_norm_w, sb_norm_w, w_out, ffn_w_gate, ffn_w_up, ffn_w_down,
              router_w, moe_w_gate, moe_w_up, moe_w_down):
    for layer in range(DEPTH):
        h = rms_norm(x, norm_mix_pre[layer])
        h = hybrid_mixer(h, w_in[layer], conv_w[layer], dn_a_log[layer], dn_dt_bias[layer],
                         dn_norm_w[layer], sb_norm_w[layer], w_out[layer])
        x = x + rms_norm(h, norm_mix_post[layer])
        h = rms_norm(x, norm_ffn_pre[layer])
        if layer % 2 == 0:
            i = layer // 2
            h = swiglu(h, ffn_w_gate[i], ffn_w_up[i], ffn_w_down[i])
        else:
            i = layer // 2
            h = moe_swiglu(h, router_w[i], moe_w_gate[i], moe_w_up[i], moe_w_down[i])
        x = x + rms_norm(h, norm_ffn_post[layer])
    return x
```

```python
import functools

import jax
import jax.numpy as jnp
from jax import lax
from jax.experimental import pallas as pl
from jax.experimental.pallas import tpu as pltpu

F32 = jnp.float32
BF16 = jnp.bfloat16
EPS = 1e-6

LANES = 128
DN_HEADS = 4
DN_HEAD_DIM = 128
DN_WIDTH = DN_HEADS * DN_HEAD_DIM
SB_HEAD_DIM = 64
SB_WIDTH = 512
SB_PAIRS = SB_WIDTH // LANES
CONV_K = 4
CHUNK = 64
N_EXPERTS = 8
VMEM_LIMIT = 56 * 1024 * 1024


def _dot(a, b):
    return jnp.dot(a, b, preferred_element_type=F32)


def _dot_nt(a, b):
    return lax.dot_general(a, b, (((1,), (1,)), ((), ())), preferred_element_type=F32)


def _sigmoid(x):
    return 1.0 / (1.0 + jnp.exp(-x))


def _softplus(x):
    return jnp.maximum(x, 0.0) + jnp.log1p(jnp.exp(-jnp.abs(x)))


def _rms_rows(x, w):
    ms = jnp.mean(x * x, axis=-1, keepdims=True)
    return x * lax.rsqrt(ms + EPS) * w


def _split3(x):
    hi = x.astype(BF16)
    r1 = x - hi.astype(F32)
    mid = r1.astype(BF16)
    lo = (r1 - mid.astype(F32)).astype(BF16)
    return hi, mid, lo


def _params(*sem):
    return pltpu.CompilerParams(dimension_semantics=sem, vmem_limit_bytes=VMEM_LIMIT)


def _in_proj_kernel(x_ref, nw_ref, w_ref, ws_ref, o_ref, os_ref, *, col_chunk):
    h = _rms_rows(x_ref[...], nw_ref[...]).astype(BF16)
    for c in range(w_ref.shape[1] // col_chunk):
        cols = slice(c * col_chunk, (c + 1) * col_chunk)
        o_ref[:, cols] = _dot(h, w_ref[:, cols]).astype(BF16)
    os_ref[...] = _dot(h, ws_ref[...])


def _in_proj(x2, nw, w_big, w_small, tm=512):
    n, d = x2.shape
    p = w_big.shape[1]
    return pl.pallas_call(
        functools.partial(_in_proj_kernel, col_chunk=512),
        grid=(n // tm,),
        in_specs=[pl.BlockSpec((tm, d), lambda i: (i, 0)),
                  pl.BlockSpec((1, d), lambda i: (0, 0)),
                  pl.BlockSpec((d, p), lambda i: (0, 0)),
                  pl.BlockSpec((d, LANES), lambda i: (0, 0))],
        out_specs=[pl.BlockSpec((tm, p), lambda i: (i, 0)),
                   pl.BlockSpec((tm, LANES), lambda i: (i, 0))],
        out_shape=[jax.ShapeDtypeStruct((n, p), BF16),
                   jax.ShapeDtypeStruct((n, LANES), F32)],
        compiler_params=_params("parallel"),
        name="in_proj",
    )(x2, nw, w_big, w_small)


def _deltanet_kernel(q_ref, k_ref, v_ref, z_ref, s_ref, cw_ref, alog_ref, dtb_ref, nw_ref,
                     o_ref, state_ref, xbuf_ref, *, tc):
    hd = DN_HEAD_DIM
    st = DN_HEADS * CHUNK

    @pl.when(pl.program_id(1) == 0)
    def _init():
        state_ref[...] = jnp.zeros_like(state_ref)
        xbuf_ref[0:8, :] = jnp.zeros((8, 3 * DN_WIDTH), F32)

    xbuf_ref[8:8 + tc, 0:DN_WIDTH] = q_ref[...].astype(F32)
    xbuf_ref[8:8 + tc, DN_WIDTH:2 * DN_WIDTH] = k_ref[...].astype(F32)
    xbuf_ref[8:8 + tc, 2 * DN_WIDTH:3 * DN_WIDTH] = v_ref[...].astype(F32)
    cw = cw_ref[...]
    acc = xbuf_ref[8:8 + tc, :] * cw[3:4, :]
    acc = acc + xbuf_ref[7:7 + tc, :] * cw[2:3, :]
    acc = acc + xbuf_ref[6:6 + tc, :] * cw[1:2, :]
    acc = acc + xbuf_ref[5:5 + tc, :] * cw[0:1, :]
    xbuf_ref[0:8, :] = xbuf_ref[tc:tc + 8, :]
    qkv = acc * _sigmoid(acc)

    def l2n(a):
        return a * lax.rsqrt(jnp.sum(a * a, axis=-1, keepdims=True) + EPS)

    qn = [l2n(qkv[:, h * hd:(h + 1) * hd]) * (hd ** -0.5) for h in range(DN_HEADS)]
    kn = [l2n(qkv[:, DN_WIDTH + h * hd:DN_WIDTH + (h + 1) * hd]) for h in range(DN_HEADS)]
    vv = [qkv[:, 2 * DN_WIDTH + h * hd:2 * DN_WIDTH + (h + 1) * hd] for h in range(DN_HEADS)]

    sm = s_ref[...]
    beta_t = _sigmoid(sm)
    g_t = -jnp.exp(alog_ref[...]) * _softplus(sm + dtb_ref[...])

    r = lax.broadcasted_iota(jnp.int32, (tc, tc), 0)
    c = lax.broadcasted_iota(jnp.int32, (tc, tc), 1)
    same = (r // CHUNK) == (c // CHUNK)
    cum_blk = jnp.where(same, jnp.where(c <= r, 1.0, 0.0), 0.0).astype(BF16)
    tot_blk = jnp.where(same, 1.0, 0.0).astype(BF16)
    g3 = jnp.concatenate(_split3(g_t), axis=1)
    red = _dot(jnp.concatenate([cum_blk, tot_blk], axis=0), g3)
    red = red[:, 0:LANES] + red[:, LANES:2 * LANES] + red[:, 2 * LANES:3 * LANES]
    gc_t, gl_t = red[0:tc], red[tc:2 * tc]

    rs = lax.broadcasted_iota(jnp.int32, (st, st), 0)
    cs = lax.broadcasted_iota(jnp.int32, (st, st), 1)
    same_s = (rs // CHUNK) == (cs // CHUNK)
    mask_incl = jnp.logical_and(same_s, cs <= rs)
    mask_strict = jnp.logical_and(same_s, cs < rs)
    eye = jnp.where(rs == cs, 1.0, 0.0).astype(F32)
    nw = nw_ref[...]

    for ci in range(tc // CHUNK):
        rows = slice(ci * CHUNK, (ci + 1) * CHUNK)

        def stack(parts):
            return jnp.concatenate([p[rows] for p in parts], axis=0)

        def stack_col(a, lane0):
            return jnp.concatenate(
                [jnp.broadcast_to(a[rows, lane0 + h:lane0 + h + 1], (CHUNK, hd))
                 for h in range(DN_HEADS)], axis=0)

        qs, ks, vs = stack(qn), stack(kn), stack(vv)
        bs = stack_col(beta_t, 0)
        gc = stack_col(gc_t, DN_HEADS)
        gl = stack_col(gl_t, DN_HEADS)
        kb = ks * bs
        vb = vs * bs
        eg = jnp.exp(gc)
        qd = qs * eg
        kbe = kb * eg
        kend = ks * jnp.exp(gl - gc)
        egl = jnp.exp(gl)

        gct = gc.T
        diff = jnp.concatenate([gc, gc], axis=1) - jnp.concatenate([gct, gct], axis=0)
        dm = jnp.exp(jnp.where(mask_incl, diff, -jnp.inf))

        ks_b = ks.astype(BF16)
        m1 = _dot_nt(jnp.concatenate([kb, qs], axis=0).astype(BF16), ks_b)
        lmat = jnp.where(mask_strict, m1[0:st] * dm, 0.0)
        attn = m1[st:2 * st] * dm

        pmat = eye - lmat
        xb = lmat.astype(BF16)
        for _ in range(5):
            xb = _dot(xb, xb).astype(BF16)
            pmat = pmat + _dot(pmat.astype(BF16), xb)
        uw = _dot(pmat.astype(BF16), jnp.concatenate([vb, kbe], axis=1).astype(BF16))
        u, wmat = uw[:, 0:hd], uw[:, hd:2 * hd].astype(BF16)

        s_cat = state_ref[...]
        s_b = s_cat.astype(BF16)
        qd_b = qd.astype(BF16)
        vn = jnp.concatenate(
            [u[h * CHUNK:(h + 1) * CHUNK]
             - _dot(wmat[h * CHUNK:(h + 1) * CHUNK], s_b[:, h * hd:(h + 1) * hd])
             for h in range(DN_HEADS)], axis=0)
        vn_b = vn.astype(BF16)
        o_intra = _dot(attn.astype(BF16), vn_b)

        head_of_row = lax.broadcasted_iota(jnp.int32, (st, hd), 0) // CHUNK
        vn_bd = jnp.concatenate(
            [jnp.where(head_of_row == h, vn_b, jnp.zeros_like(vn_b)) for h in range(DN_HEADS)],
            axis=1)
        decay = jnp.concatenate(
            [jnp.concatenate([egl[h * CHUNK:(h + 1) * CHUNK]] * (hd // CHUNK), axis=0)
             for h in range(DN_HEADS)], axis=1)
        state_ref[...] = s_cat * decay + _dot(kend.T.astype(BF16), vn_bd)

        for h in range(DN_HEADS):
            hs = slice(h * CHUNK, (h + 1) * CHUNK)
            o_h = _dot(qd_b[hs], s_b[:, h * hd:(h + 1) * hd]) + o_intra[hs]
            zh = z_ref[rows, h * hd:(h + 1) * hd].astype(F32)
            o_ref[rows, h * hd:(h + 1) * hd] = (
                _rms_rows(o_h, nw) * (zh * _sigmoid(zh))).astype(BF16)


def _deltanet(proj, small, conv_w, alog_pad, dtb_pad, dn_norm_w, batch, seq, tc=256):
    n = proj.shape[0]
    nt = seq // tc
    row = lambda b, t: b * nt + t
    return pl.pallas_call(
        functools.partial(_deltanet_kernel, tc=tc),
        grid=(batch, nt),
        in_specs=[pl.BlockSpec((tc, DN_WIDTH), lambda b, t: (row(b, t), 0)),
                  pl.BlockSpec((tc, DN_WIDTH), lambda b, t: (row(b, t), 1)),
                  pl.BlockSpec((tc, DN_WIDTH), lambda b, t: (row(b, t), 2)),
                  pl.BlockSpec((tc, DN_WIDTH), lambda b, t: (row(b, t), 3)),
                  pl.BlockSpec((tc, LANES), lambda b, t: (row(b, t), 0)),
                  pl.BlockSpec((CONV_K, 3 * DN_WIDTH), lambda b, t: (0, 0)),
                  pl.BlockSpec((1, LANES), lambda b, t: (0, 0)),
                  pl.BlockSpec((1, LANES), lambda b, t: (0, 0)),
                  pl.BlockSpec((1, DN_HEAD_DIM), lambda b, t: (0, 0))],
        out_specs=pl.BlockSpec((tc, DN_WIDTH), lambda b, t: (row(b, t), 0)),
        out_shape=jax.ShapeDtypeStruct((n, DN_WIDTH), BF16),
        scratch_shapes=[pltpu.VMEM((DN_HEAD_DIM, DN_WIDTH), F32),
                        pltpu.VMEM((tc + 8, 3 * DN_WIDTH), F32)],
        compiler_params=_params("parallel", "arbitrary"),
        name="deltanet",
    )(proj, proj, proj, proj, small, conv_w, alog_pad, dtb_pad, dn_norm_w)


def _sb_kernel(q_ref, k_ref, v_ref, nw_ref, o_ref, *, tq):
    tk = tq
    i = pl.program_id(2)
    lane = lax.broadcasted_iota(jnp.int32, (1, LANES), 1)
    first = lane < SB_HEAD_DIM
    q2 = q_ref[...] * (SB_HEAD_DIM ** -0.5)
    zero = jnp.zeros_like(q2)
    q_heads = (jnp.where(first, q2, zero), jnp.where(first, zero, q2))

    rj = lax.broadcasted_iota(jnp.int32, (tk, tk + LANES), 0)
    cs = lax.broadcasted_iota(jnp.int32, (tk, tk + LANES), 1)
    suffix = jnp.where(cs >= tk, 1.0, jnp.where(rj >= cs, 1.0, 0.0)).astype(BF16)
    suffix2 = jnp.concatenate([suffix, suffix], axis=0)

    rq = lax.broadcasted_iota(jnp.int32, (tq, tk), 0)
    ck = lax.broadcasted_iota(jnp.int32, (tq, tk), 1)
    causal = ck < rq

    def tile(j, carry, acc, diagonal):
        start = pl.multiple_of(j * tk, tk)
        kb = k_ref[pl.ds(start, tk), :]
        vb = v_ref[pl.ds(start, tk), :]
        new_carry, new_acc = [], []
        for a in range(2):
            z = _dot_nt(q_heads[a], kb)
            log_keep = -_softplus(z)
            if diagonal:
                log_keep = jnp.where(causal, log_keep, 0.0)
            hi = log_keep.astype(BF16)
            lo = (log_keep - hi.astype(F32)).astype(BF16)
            inc = _dot(jnp.concatenate([hi, lo], axis=1), suffix2)
            logw = z + inc[:, 0:tk] + jnp.concatenate([carry[a]] * (tk // LANES), axis=1)
            w = jnp.exp(logw)
            if diagonal:
                w = jnp.where(causal, w, 0.0)
            new_acc.append(acc[a] + _dot(w.astype(BF16), vb))
            new_carry.append(carry[a] + inc[:, tk:tk + LANES])
        return tuple(new_carry), tuple(new_acc)

    zeros = jnp.zeros((tq, LANES), F32)
    carry, acc = tile(i, (zeros, zeros), (zeros, zeros), True)

    def body(jj, state):
        return tile(i - 1 - jj, state[0], state[1], False)

    carry, acc = lax.fori_loop(0, i, body, (carry, acc))

    o2 = jnp.where(first, acc[0], acc[1])
    sq = o2 * o2
    s_first = jnp.sum(jnp.where(first, sq, 0.0), axis=-1, keepdims=True)
    s_all = jnp.sum(sq, axis=-1, keepdims=True)
    ms = jnp.where(first, s_first, s_all - s_first) * (1.0 / SB_HEAD_DIM)
    o_ref[...] = (o2 * lax.rsqrt(ms + EPS) * nw_ref[...]).astype(BF16)


def _sb_attention(proj, sb_norm_w2, batch, seq, col0, tq=256):
    n = proj.shape[0]
    nq = seq // tq
    return pl.pallas_call(
        functools.partial(_sb_kernel, tq=tq),
        grid=(batch, SB_PAIRS, nq),
        in_specs=[pl.BlockSpec((tq, LANES), lambda b, p, i: (b * nq + i, col0 + p)),
                  pl.BlockSpec((seq, LANES), lambda b, p, i: (b, col0 + SB_PAIRS + p)),
                  pl.BlockSpec((seq, LANES), lambda b, p, i: (b, col0 + 2 * SB_PAIRS + p)),
                  pl.BlockSpec((1, LANES), lambda b, p, i: (0, 0))],
        out_specs=pl.BlockSpec((tq, LANES), lambda b, p, i: (b * nq + i, p)),
        out_shape=jax.ShapeDtypeStruct((n, SB_WIDTH), BF16),
        compiler_params=_params("parallel", "parallel", "arbitrary"),
        name="sb_attention",
    )(proj, proj, proj, sb_norm_w2)


def _out_proj_kernel(dn_ref, sb_ref, x_ref, w_ref, nw_ref, o_ref):
    half = dn_ref.shape[1]
    mix = _dot(dn_ref[...], w_ref[0:half, :]) + _dot(sb_ref[...], w_ref[half:2 * half, :])
    o_ref[...] = x_ref[...] + _rms_rows(mix, nw_ref[...])


def _out_proj(dn, sb, x2, w_out, nw, tm=512):
    n, d = x2.shape
    half = dn.shape[1]
    return pl.pallas_call(
        _out_proj_kernel,
        grid=(n // tm,),
        in_specs=[pl.BlockSpec((tm, half), lambda i: (i, 0)),
                  pl.BlockSpec((tm, half), lambda i: (i, 0)),
                  pl.BlockSpec((tm, d), lambda i: (i, 0)),
                  pl.BlockSpec((2 * half, d), lambda i: (0, 0)),
                  pl.BlockSpec((1, d), lambda i: (0, 0))],
        out_specs=pl.BlockSpec((tm, d), lambda i: (i, 0)),
        out_shape=jax.ShapeDtypeStruct((n, d), F32),
        compiler_params=_params("parallel"),
        name="out_proj",
    )(dn, sb, x2, w_out, nw)


def _dense_ffn_kernel(x_ref, pre_ref, post_ref, wg_ref, wu_ref, wd_ref, o_ref, h_ref, acc_ref):
    j = pl.program_id(1)

    @pl.when(j == 0)
    def _():
        h_ref[...] = _rms_rows(x_ref[...], pre_ref[...]).astype(BF16)

    h = h_ref[...]
    g = _dot(h, wg_ref[...])
    u = _dot(h, wu_ref[...])
    d = _dot((g * _sigmoid(g) * u).astype(BF16), wd_ref[...])

    @pl.when(j == 0)
    def _():
        acc_ref[...] = d

    @pl.when(j > 0)
    def _():
        acc_ref[...] += d

    @pl.when(j == pl.num_programs(1) - 1)
    def _():
        o_ref[...] = x_ref[...] + _rms_rows(acc_ref[...], post_ref[...])


def _dense_ffn(x2, pre, post, wg, wu, wd, tm=512, tf=1408):
    n, d = x2.shape
    ff = wg.shape[1]
    return pl.pallas_call(
        _dense_ffn_kernel,
        grid=(n // tm, ff // tf),
        in_specs=[pl.BlockSpec((tm, d), lambda i, j: (i, 0)),
                  pl.BlockSpec((1, d), lambda i, j: (0, 0)),
                  pl.BlockSpec((1, d), lambda i, j: (0, 0)),
                  pl.BlockSpec((d, tf), lambda i, j: (0, j)),
                  pl.BlockSpec((d, tf), lambda i, j: (0, j)),
                  pl.BlockSpec((tf, d), lambda i, j: (j, 0))],
        out_specs=pl.BlockSpec((tm, d), lambda i, j: (i, 0)),
        out_shape=jax.ShapeDtypeStruct((n, d), F32),
        scratch_shapes=[pltpu.VMEM((tm, d), BF16), pltpu.VMEM((tm, d), F32)],
        compiler_params=_params("parallel", "arbitrary"),
        name="dense_ffn",
    )(x2, pre, post, wg, wu, wd)


def _router_kernel(x_ref, nw_ref, rhi_ref, rlo_ref, h_ref, meta_ref, cnt_ref, run_ref):
    tm = x_ref.shape[0]

    @pl.when(pl.program_id(0) == 0)
    def _():
        run_ref[...] = jnp.zeros_like(run_ref)

    hf = _rms_rows(x_ref[...], nw_ref[...])
    h_ref[...] = hf
    hb = hf.astype(BF16)
    hlo = (hf - hb.astype(F32)).astype(BF16)
    logits = _dot(hb, rhi_ref[...]) + _dot(hb, rlo_ref[...]) + _dot(hlo, rhi_ref[...])

    lane = lax.broadcasted_iota(jnp.int32, (tm, LANES), 1).astype(F32)
    lg = jnp.where(lane < N_EXPERTS, logits, -jnp.inf)
    m1 = jnp.max(lg, axis=-1, keepdims=True)
    i1 = jnp.min(jnp.where(lg == m1, lane, float(LANES)), axis=-1, keepdims=True)
    lg2 = jnp.where(lane == i1, -jnp.inf, lg)
    m2 = jnp.max(lg2, axis=-1, keepdims=True)
    i2 = jnp.min(jnp.where(lg2 == m2, lane, float(LANES)), axis=-1, keepdims=True)
    e = jnp.exp(m2 - m1)
    g1 = 1.0 / (1.0 + e)
    g2 = e * g1

    onehot = jnp.where(lane == i1, 1.0, jnp.where(lane == i2, 1.0, 0.0))
    r = lax.broadcasted_iota(jnp.int32, (tm, tm), 0)
    c = lax.broadcasted_iota(jnp.int32, (tm, tm), 1)
    before = jnp.where(c < r, 1.0, 0.0).astype(BF16)
    rank = _dot(before, onehot.astype(BF16)) + run_ref[...]
    r1 = jnp.sum(jnp.where(lane == i1, rank, 0.0), axis=-1, keepdims=True)
    r2 = jnp.sum(jnp.where(lane == i2, rank, 0.0), axis=-1, keepdims=True)
    run_ref[...] += jnp.sum(onehot, axis=0, keepdims=True)
    cnt_ref[...] = run_ref[...]

    meta = jnp.zeros((tm, LANES), F32)
    for k, val in enumerate((i1, i2, g1, g2, r1, r2)):
        meta = jnp.where(lane == float(k), val, meta)
    meta_ref[...] = meta


def _router(x2, nw, r_hi, r_lo, tm=512):
    n, d = x2.shape
    return pl.pallas_call(
        _router_kernel,
        grid=(n // tm,),
        in_specs=[pl.BlockSpec((tm, d), lambda i: (i, 0)),
                  pl.BlockSpec((1, d), lambda i: (0, 0)),
                  pl.BlockSpec((d, LANES), lambda i: (0, 0)),
                  pl.BlockSpec((d, LANES), lambda i: (0, 0))],
        out_specs=[pl.BlockSpec((tm, d), lambda i: (i, 0)),
                   pl.BlockSpec((tm, LANES), lambda i: (i, 0)),
                   pl.BlockSpec((1, LANES), lambda i: (0, 0))],
        out_shape=[jax.ShapeDtypeStruct((n, d), F32),
                   jax.ShapeDtypeStruct((n, LANES), F32),
                   jax.ShapeDtypeStruct((1, LANES), F32)],
        scratch_shapes=[pltpu.VMEM((1, LANES), F32)],
        compiler_params=_params("arbitrary"),
        name="router",
    )(x2, nw, r_hi, r_lo)


def _row_copy(src_ref, src_row, dst_ref, dst_row, sem):
    return pltpu.make_async_copy(src_ref.at[pl.ds(src_row, 1)], dst_ref.at[pl.ds(dst_row, 1)], sem)


def _dispatch_kernel(pos_ref, h_ref, xs_in_ref, xs_ref, sem, *, tb):
    del xs_in_ref
    base = pl.program_id(0) * tb

    def issue(t, carry):
        _row_copy(h_ref, base + t, xs_ref, pos_ref[2 * t], sem).start()
        _row_copy(h_ref, base + t, xs_ref, pos_ref[2 * t + 1], sem).start()
        return carry

    lax.fori_loop(0, tb, issue, 0)

    def drain(t, carry):
        _row_copy(h_ref, base + t, xs_ref, pos_ref[2 * t], sem).wait()
        _row_copy(h_ref, base + t, xs_ref, pos_ref[2 * t + 1], sem).wait()
        return carry

    lax.fori_loop(0, tb, drain, 0)


def _dispatch(pos_flat, h, xs_zero, tb=512):
    n = h.shape[0]
    return pl.pallas_call(
        functools.partial(_dispatch_kernel, tb=tb),
        grid=(n // tb,),
        in_specs=[pl.BlockSpec((2 * tb,), lambda i: (i,), memory_space=pltpu.SMEM),
                  pl.BlockSpec(memory_space=pl.ANY),
                  pl.BlockSpec(memory_space=pl.ANY)],
        out_specs=pl.BlockSpec(memory_space=pl.ANY),
        out_shape=jax.ShapeDtypeStruct(xs_zero.shape, xs_zero.dtype),
        scratch_shapes=[pltpu.SemaphoreType.DMA(())],
        input_output_aliases={2: 0},
        compiler_params=_params("arbitrary"),
        name="moe_dispatch",
    )(pos_flat, h, xs_zero)


def _moe_ffn_kernel(te_ref, nv_ref, xs_ref, wg_ref, wu_ref, wd_ref, ys_ref, h_ref, acc_ref):
    del te_ref
    i = pl.program_id(0)
    j = pl.program_id(1)
    last = pl.num_programs(1) - 1
    valid = i < nv_ref[0]

    @pl.when(valid)
    def _():
        @pl.when(j == 0)
        def _():
            h_ref[...] = xs_ref[...].astype(BF16)

        h = h_ref[...]
        g = _dot(h, wg_ref[0])
        u = _dot(h, wu_ref[0])
        d = _dot((g * _sigmoid(g) * u).astype(BF16), wd_ref[0])

        @pl.when(j == 0)
        def _():
            acc_ref[...] = d

        @pl.when(j > 0)
        def _():
            acc_ref[...] += d

        @pl.when(j == last)
        def _():
            ys_ref[...] = acc_ref[...]

    @pl.when(jnp.logical_and(jnp.logical_not(valid), j == last))
    def _():
        ys_ref[...] = jnp.zeros_like(ys_ref)


def _moe_ffn(tile_expert, n_valid, xs, wg, wu, wd, tm, tf):
    rows, d = xs.shape
    ff = wg.shape[2]
    grid_spec = pltpu.PrefetchScalarGridSpec(
        num_scalar_prefetch=2,
        grid=(rows // tm, ff // tf),
        in_specs=[pl.BlockSpec((tm, d), lambda i, j, te, nv: (i, 0)),
                  pl.BlockSpec((1, d, tf), lambda i, j, te, nv: (te[i], 0, j)),
                  pl.BlockSpec((1, d, tf), lambda i, j, te, nv: (te[i], 0, j)),
                  pl.BlockSpec((1, tf, d), lambda i, j, te, nv: (te[i], j, 0))],
        out_specs=pl.BlockSpec((tm, d), lambda i, j, te, nv: (i, 0)),
        scratch_shapes=[pltpu.VMEM((tm, d), BF16), pltpu.VMEM((tm, d), F32)],
    )
    return pl.pallas_call(
        _moe_ffn_kernel,
        grid_spec=grid_spec,
        out_shape=jax.ShapeDtypeStruct((rows, d), F32),
        compiler_params=_params("arbitrary", "arbitrary"),
        name="moe_ffn",
    )(tile_expert, n_valid, xs, wg, wu, wd)


def _combine_kernel(pos_ref, ys_ref, x_ref, meta_ref, nw_ref, o_ref, y0_ref, y1_ref, sem, *, tb):
    def issue(t, carry):
        _row_copy(ys_ref, pos_ref[2 * t], y0_ref, t, sem).start()
        _row_copy(ys_ref, pos_ref[2 * t + 1], y1_ref, t, sem).start()
        return carry

    lax.fori_loop(0, tb, issue, 0)

    def drain(t, carry):
        _row_copy(ys_ref, pos_ref[2 * t], y0_ref, t, sem).wait()
        _row_copy(ys_ref, pos_ref[2 * t + 1], y1_ref, t, sem).wait()
        return carry

    lax.fori_loop(0, tb, drain, 0)

    meta = meta_ref[...]
    y = meta[:, 2:3] * y0_ref[...] + meta[:, 3:4] * y1_ref[...]
    o_ref[...] = x_ref[...] + _rms_rows(y, nw_ref[...])


def _combine(pos_flat, ys, x2, meta, nw, tb=512):
    n, d = x2.shape
    return pl.pallas_call(
        functools.partial(_combine_kernel, tb=tb),
        grid=(n // tb,),
        in_specs=[pl.BlockSpec((2 * tb,), lambda i: (i,), memory_space=pltpu.SMEM),
                  pl.BlockSpec(memory_space=pl.ANY),
                  pl.BlockSpec((tb, d), lambda i: (i, 0)),
                  pl.BlockSpec((tb, LANES), lambda i: (i, 0)),
                  pl.BlockSpec((1, d), lambda i: (0, 0))],
        out_specs=pl.BlockSpec((tb, d), lambda i: (i, 0)),
        out_shape=jax.ShapeDtypeStruct((n, d), F32),
        scratch_shapes=[pltpu.VMEM((tb, d), F32), pltpu.VMEM((tb, d), F32),
                        pltpu.SemaphoreType.DMA(())],
        compiler_params=_params("arbitrary"),
        name="moe_combine",
    )(pos_flat, ys, x2, meta, nw)


def _moe_layer(x2, pre, post, router_w, wg, wu, wd, tm=512, tf=1792):
    n, d = x2.shape
    r_pad = jnp.pad(router_w, ((0, 0), (0, LANES - N_EXPERTS)))
    r_hi = r_pad.astype(BF16)
    r_lo = (r_pad - r_hi.astype(F32)).astype(BF16)
    h, meta, cnt = _router(x2, pre, r_hi, r_lo)

    idx = meta[:, 0:2].astype(jnp.int32)
    rank = meta[:, 4:6].astype(jnp.int32)
    counts = cnt[0, 0:N_EXPERTS].astype(jnp.int32)
    padded = ((counts + tm - 1) // tm) * tm
    ends = jnp.cumsum(padded)
    offsets = ends - padded
    pos_flat = (offsets[idx] + rank).reshape(-1)
    n_tiles = (2 * n) // tm + N_EXPERTS
    tile_start = jnp.arange(n_tiles, dtype=jnp.int32) * tm
    tile_expert = jnp.minimum(
        jnp.sum((tile_start[:, None] >= ends[None, :]).astype(jnp.int32), axis=1), N_EXPERTS - 1)
    n_valid = (ends[-1:] // tm).astype(jnp.int32)

    xs = _dispatch(pos_flat, h, jnp.zeros((n_tiles * tm, d), F32))
    ys = _moe_ffn(tile_expert, n_valid, xs, wg.astype(BF16), wu.astype(BF16), wd.astype(BF16),
                  tm, tf)
    return _combine(pos_flat, ys, x2, meta, post)


def kernel(x, norm_mix_pre, norm_mix_post, norm_ffn_pre, norm_ffn_post, w_in, conv_w, dn_a_log,
           dn_dt_bias, dn_norm_w, sb_norm_w, w_out, ffn_w_gate, ffn_w_up, ffn_w_down, router_w,
           moe_w_gate, moe_w_up, moe_w_down):
    batch, seq, d = x.shape
    depth = w_in.shape[0]
    x2 = x.reshape(batch * seq, d)
    gate_cols = 4 * DN_WIDTH
    n_small = 2 * DN_HEADS
    sb_col0 = gate_cols // LANES

    def pad_lanes(v, lane0):
        return jnp.zeros((1, LANES), F32).at[0, lane0:lane0 + v.shape[0]].set(v)

    for layer in range(depth):
        w = w_in[layer]
        w_big = jnp.concatenate([w[:, :gate_cols], w[:, gate_cols + n_small:]], axis=1).astype(BF16)
        w_small = jnp.pad(w[:, gate_cols:gate_cols + n_small],
                          ((0, 0), (0, LANES - n_small))).astype(BF16)
        proj, small = _in_proj(x2, norm_mix_pre[layer][None, :], w_big, w_small)
        dn = _deltanet(proj, small, conv_w[layer], pad_lanes(dn_a_log[layer], DN_HEADS),
                       pad_lanes(dn_dt_bias[layer], DN_HEADS), dn_norm_w[layer][None, :],
                       batch, seq)
        sb = _sb_attention(proj, jnp.tile(sb_norm_w[layer], 2)[None, :], batch, seq, sb_col0)
        x2 = _out_proj(dn, sb, x2, w_out[layer].astype(BF16), norm_mix_post[layer][None, :])

        pre = norm_ffn_pre[layer][None, :]
        post = norm_ffn_post[layer][None, :]
        i = layer // 2
        if layer % 2 == 0:
            x2 = _dense_ffn(x2, pre, post, ffn_w_gate[i].astype(BF16), ffn_w_up[i].astype(BF16),
                            ffn_w_down[i].astype(BF16))
        else:
            x2 = _moe_layer(x2, pre, post, router_w[i], moe_w_gate[i], moe_w_up[i], moe_w_down[i])
    return x2.reshape(batch, seq, d)
```

```python
import functools

import jax
import jax.numpy as jnp
from jax import lax
from jax.experimental import pallas as pl
from jax.experimental.pallas import tpu as pltpu

F32 = jnp.float32
BF16 = jnp.bfloat16
EPS = 1e-6
LOG2E = 1.4426950408889634

LANES = 128
DN_HEADS = 4
DN_HEAD_DIM = 128
DN_WIDTH = DN_HEADS * DN_HEAD_DIM
SB_HEAD_DIM = 64
SB_WIDTH = 512
SB_PAIRS = SB_WIDTH // LANES
CONV_K = 4
CHUNK = 64
N_EXPERTS = 8
ROW_DMA_UNROLL = 8
VMEM_LIMIT = 56 * 1024 * 1024


def _dot(a, b):
    return jnp.dot(a, b, preferred_element_type=F32)


def _dot_nt(a, b):
    return lax.dot_general(a, b, (((1,), (1,)), ((), ())), preferred_element_type=F32)


def _sigmoid(x):
    return 1.0 / (1.0 + jnp.exp(-x))


def _softplus(x):
    return jnp.maximum(x, 0.0) + jnp.log1p(jnp.exp(-jnp.abs(x)))


def _rms_rows(x, w):
    ms = jnp.mean(x * x, axis=-1, keepdims=True)
    return x * lax.rsqrt(ms + EPS) * w


def _split3(x):
    hi = x.astype(BF16)
    r1 = x - hi.astype(F32)
    mid = r1.astype(BF16)
    lo = (r1 - mid.astype(F32)).astype(BF16)
    return hi, mid, lo


def _params(*sem):
    return pltpu.CompilerParams(dimension_semantics=sem, vmem_limit_bytes=VMEM_LIMIT)


def _in_proj_kernel(x_ref, nw_ref, w_ref, ws_ref, o_ref, os_ref, *, col_chunk):
    h = _rms_rows(x_ref[...], nw_ref[...]).astype(BF16)
    for c in range(w_ref.shape[1] // col_chunk):
        cols = slice(c * col_chunk, (c + 1) * col_chunk)
        o_ref[:, cols] = _dot(h, w_ref[:, cols]).astype(BF16)
    os_ref[...] = _dot(h, ws_ref[...])


def _in_proj(x2, nw, w_big, w_small, tm=512):
    n, d = x2.shape
    p = w_big.shape[1]
    return pl.pallas_call(
        functools.partial(_in_proj_kernel, col_chunk=512),
        grid=(n // tm,),
        in_specs=[pl.BlockSpec((tm, d), lambda i: (i, 0)),
                  pl.BlockSpec((1, d), lambda i: (0, 0)),
                  pl.BlockSpec((d, p), lambda i: (0, 0)),
                  pl.BlockSpec((d, LANES), lambda i: (0, 0))],
        out_specs=[pl.BlockSpec((tm, p), lambda i: (i, 0)),
                   pl.BlockSpec((tm, LANES), lambda i: (i, 0))],
        out_shape=[jax.ShapeDtypeStruct((n, p), BF16),
                   jax.ShapeDtypeStruct((n, LANES), F32)],
        compiler_params=_params("parallel"),
        name="in_proj",
    )(x2, nw, w_big, w_small)


def _deltanet_kernel(q_ref, k_ref, v_ref, z_ref, s_ref, cw_ref, alog_ref, dtb_ref, nw_ref,
                     o_ref, state_ref, xbuf_ref, *, tc):
    hd = DN_HEAD_DIM
    st = DN_HEADS * CHUNK

    @pl.when(pl.program_id(1) == 0)
    def _init():
        state_ref[...] = jnp.zeros_like(state_ref)
        xbuf_ref[0:8, :] = jnp.zeros((8, 3 * DN_WIDTH), F32)

    xbuf_ref[8:8 + tc, 0:DN_WIDTH] = q_ref[...].astype(F32)
    xbuf_ref[8:8 + tc, DN_WIDTH:2 * DN_WIDTH] = k_ref[...].astype(F32)
    xbuf_ref[8:8 + tc, 2 * DN_WIDTH:3 * DN_WIDTH] = v_ref[...].astype(F32)
    cw = cw_ref[...]
    acc = xbuf_ref[8:8 + tc, :] * cw[3:4, :]
    acc = acc + xbuf_ref[7:7 + tc, :] * cw[2:3, :]
    acc = acc + xbuf_ref[6:6 + tc, :] * cw[1:2, :]
    acc = acc + xbuf_ref[5:5 + tc, :] * cw[0:1, :]
    xbuf_ref[0:8, :] = xbuf_ref[tc:tc + 8, :]
    qkv = acc * _sigmoid(acc)

    def l2n(a):
        return a * lax.rsqrt(jnp.sum(a * a, axis=-1, keepdims=True) + EPS)

    qn = [l2n(qkv[:, h * hd:(h + 1) * hd]) * (hd ** -0.5) for h in range(DN_HEADS)]
    kn = [l2n(qkv[:, DN_WIDTH + h * hd:DN_WIDTH + (h + 1) * hd]) for h in range(DN_HEADS)]
    vv = [qkv[:, 2 * DN_WIDTH + h * hd:2 * DN_WIDTH + (h + 1) * hd] for h in range(DN_HEADS)]

    sm = s_ref[...]
    beta_t = _sigmoid(sm)
    g_t = -jnp.exp(alog_ref[...]) * _softplus(sm + dtb_ref[...])

    r = lax.broadcasted_iota(jnp.int32, (tc, tc), 0)
    c = lax.broadcasted_iota(jnp.int32, (tc, tc), 1)
    same = (r // CHUNK) == (c // CHUNK)
    cum_blk = jnp.where(same, jnp.where(c <= r, 1.0, 0.0), 0.0).astype(BF16)
    tot_blk = jnp.where(same, 1.0, 0.0).astype(BF16)
    g3 = jnp.concatenate(_split3(g_t), axis=1)
    red = _dot(jnp.concatenate([cum_blk, tot_blk], axis=0), g3)
    red = red[:, 0:LANES] + red[:, LANES:2 * LANES] + red[:, 2 * LANES:3 * LANES]
    gc_t, gl_t = red[0:tc], red[tc:2 * tc]

    rs = lax.broadcasted_iota(jnp.int32, (st, st), 0)
    cs = lax.broadcasted_iota(jnp.int32, (st, st), 1)
    same_s = (rs // CHUNK) == (cs // CHUNK)
    mask_incl = jnp.logical_and(same_s, cs <= rs)
    mask_strict = jnp.logical_and(same_s, cs < rs)
    eye = jnp.where(rs == cs, 1.0, 0.0).astype(F32)
    nw = nw_ref[...]

    for ci in range(tc // CHUNK):
        rows = slice(ci * CHUNK, (ci + 1) * CHUNK)

        def stack(parts):
            return jnp.concatenate([p[rows] for p in parts], axis=0)

        def stack_col(a, lane0):
            return jnp.concatenate(
                [jnp.broadcast_to(a[rows, lane0 + h:lane0 + h + 1], (CHUNK, hd))
                 for h in range(DN_HEADS)], axis=0)

        qs, ks, vs = stack(qn), stack(kn), stack(vv)
        bs = stack_col(beta_t, 0)
        gc = stack_col(gc_t, DN_HEADS)
        gl = stack_col(gl_t, DN_HEADS)
        kb = ks * bs
        vb = vs * bs
        eg = jnp.exp(gc)
        qd = qs * eg
        kbe = kb * eg
        kend = ks * jnp.exp(gl - gc)
        egl = jnp.exp(gl)

        gct = gc.T
        diff = jnp.concatenate([gc, gc], axis=1) - jnp.concatenate([gct, gct], axis=0)
        dm = jnp.exp(jnp.where(mask_incl, diff, -jnp.inf))

        ks_b = ks.astype(BF16)
        m1 = _dot_nt(jnp.concatenate([kb, qs], axis=0).astype(BF16), ks_b)
        lmat = jnp.where(mask_strict, m1[0:st] * dm, 0.0)
        attn = m1[st:2 * st] * dm

        pmat = eye - lmat
        xb = lmat.astype(BF16)
        for _ in range(5):
            xb = _dot(xb, xb).astype(BF16)
            pmat = pmat + _dot(pmat.astype(BF16), xb)
        uw = _dot(pmat.astype(BF16), jnp.concatenate([vb, kbe], axis=1).astype(BF16))
        u, wmat = uw[:, 0:hd], uw[:, hd:2 * hd].astype(BF16)

        s_cat = state_ref[...]
        s_b = s_cat.astype(BF16)
        qd_b = qd.astype(BF16)
        vn = jnp.concatenate(
            [u[h * CHUNK:(h + 1) * CHUNK]
             - _dot(wmat[h * CHUNK:(h + 1) * CHUNK], s_b[:, h * hd:(h + 1) * hd])
             for h in range(DN_HEADS)], axis=0)
        vn_b = vn.astype(BF16)
        o_intra = _dot(attn.astype(BF16), vn_b)

        head_of_row = lax.broadcasted_iota(jnp.int32, (st, hd), 0) // CHUNK
        vn_bd = jnp.concatenate(
            [jnp.where(head_of_row == h, vn_b, jnp.zeros_like(vn_b)) for h in range(DN_HEADS)],
            axis=1)
        decay = jnp.concatenate(
            [jnp.concatenate([egl[h * CHUNK:(h + 1) * CHUNK]] * (hd // CHUNK), axis=0)
             for h in range(DN_HEADS)], axis=1)
        state_ref[...] = s_cat * decay + _dot(kend.T.astype(BF16), vn_bd)

        for h in range(DN_HEADS):
            hs = slice(h * CHUNK, (h + 1) * CHUNK)
            o_h = _dot(qd_b[hs], s_b[:, h * hd:(h + 1) * hd]) + o_intra[hs]
            zh = z_ref[rows, h * hd:(h + 1) * hd].astype(F32)
            o_ref[rows, h * hd:(h + 1) * hd] = (
                _rms_rows(o_h, nw) * (zh * _sigmoid(zh))).astype(BF16)


def _deltanet(proj, small, conv_w, alog_pad, dtb_pad, dn_norm_w, batch, seq, tc=256):
    n = proj.shape[0]
    nt = seq // tc
    row = lambda b, t: b * nt + t
    return pl.pallas_call(
        functools.partial(_deltanet_kernel, tc=tc),
        grid=(batch, nt),
        in_specs=[pl.BlockSpec((tc, DN_WIDTH), lambda b, t: (row(b, t), 0)),
                  pl.BlockSpec((tc, DN_WIDTH), lambda b, t: (row(b, t), 1)),
                  pl.BlockSpec((tc, DN_WIDTH), lambda b, t: (row(b, t), 2)),
                  pl.BlockSpec((tc, DN_WIDTH), lambda b, t: (row(b, t), 3)),
                  pl.BlockSpec((tc, LANES), lambda b, t: (row(b, t), 0)),
                  pl.BlockSpec((CONV_K, 3 * DN_WIDTH), lambda b, t: (0, 0)),
                  pl.BlockSpec((1, LANES), lambda b, t: (0, 0)),
                  pl.BlockSpec((1, LANES), lambda b, t: (0, 0)),
                  pl.BlockSpec((1, DN_HEAD_DIM), lambda b, t: (0, 0))],
        out_specs=pl.BlockSpec((tc, DN_WIDTH), lambda b, t: (row(b, t), 0)),
        out_shape=jax.ShapeDtypeStruct((n, DN_WIDTH), BF16),
        scratch_shapes=[pltpu.VMEM((DN_HEAD_DIM, DN_WIDTH), F32),
                        pltpu.VMEM((tc + 8, 3 * DN_WIDTH), F32)],
        compiler_params=_params("parallel", "arbitrary"),
        name="deltanet",
    )(proj, proj, proj, proj, small, conv_w, alog_pad, dtb_pad, dn_norm_w)


def _sb_kernel(q_ref, k_ref, v_ref, nw_ref, o_ref, *, tq):
    tk = tq
    i = pl.program_id(2)
    lane = lax.broadcasted_iota(jnp.int32, (1, LANES), 1)
    first = lane < SB_HEAD_DIM
    q2 = q_ref[...] * (SB_HEAD_DIM ** -0.5)
    zero = jnp.zeros_like(q2)
    q_st = jnp.concatenate([jnp.where(first, q2, zero), jnp.where(first, zero, q2)], axis=0)

    rj = lax.broadcasted_iota(jnp.int32, (tk, tk), 0)
    cs = lax.broadcasted_iota(jnp.int32, (tk, tk), 1)
    suffix = jnp.where(rj >= cs, 1.0, 0.0).astype(BF16)
    suffix2 = jnp.concatenate([suffix, suffix], axis=0)

    rq = lax.broadcasted_iota(jnp.int32, (2 * tq, tk), 0)
    ck = lax.broadcasted_iota(jnp.int32, (2 * tq, tk), 1)
    causal = ck < jnp.where(rq >= tq, rq - tq, rq)
    last_lane = lane == LANES - 1

    def tile(j, total, acc, diagonal):
        start = pl.multiple_of(j * tk, tk)
        kb = k_ref[pl.ds(start, tk), :]
        vb = v_ref[pl.ds(start, tk), :]
        z = _dot_nt(q_st, kb)
        drop = jnp.maximum(z, 0.0) + jnp.log(1.0 + jnp.exp2(jnp.abs(z) * -LOG2E))
        if diagonal:
            drop = jnp.where(causal, drop, 0.0)
        else:
            parked = jnp.where(last_lane, pltpu.roll(total, LANES - 1, 1), 0.0)
            drop = jnp.concatenate([drop[:, 0:tk - LANES], drop[:, tk - LANES:tk] + parked],
                                   axis=1)
        hi = drop.astype(BF16)
        lo = (drop - hi.astype(F32)).astype(BF16)
        inc = _dot(jnp.concatenate([hi, lo], axis=1), suffix2)
        w = jnp.exp2((z - inc) * LOG2E)
        if diagonal:
            w = jnp.where(causal, w, 0.0)
        return inc[:, 0:LANES], acc + _dot(w.astype(BF16), vb)

    zeros = jnp.zeros((2 * tq, LANES), F32)
    total, acc = tile(i, zeros, zeros, True)

    def body(jj, state):
        return tile(i - 1 - jj, state[0], state[1], False)

    total, acc = lax.fori_loop(0, i, body, (total, acc))

    o2 = jnp.where(first, acc[0:tq], acc[tq:2 * tq])
    sq = o2 * o2
    s_first = jnp.sum(jnp.where(first, sq, 0.0), axis=-1, keepdims=True)
    s_all = jnp.sum(sq, axis=-1, keepdims=True)
    ms = jnp.where(first, s_first, s_all - s_first) * (1.0 / SB_HEAD_DIM)
    o_ref[...] = (o2 * lax.rsqrt(ms + EPS) * nw_ref[...]).astype(BF16)


def _sb_attention(proj, sb_norm_w2, batch, seq, col0, tq=256):
    n = proj.shape[0]
    nq = seq // tq
    return pl.pallas_call(
        functools.partial(_sb_kernel, tq=tq),
        grid=(batch, SB_PAIRS, nq),
        in_specs=[pl.BlockSpec((tq, LANES), lambda b, p, i: (b * nq + i, col0 + p)),
                  pl.BlockSpec((seq, LANES), lambda b, p, i: (b, col0 + SB_PAIRS + p)),
                  pl.BlockSpec((seq, LANES), lambda b, p, i: (b, col0 + 2 * SB_PAIRS + p)),
                  pl.BlockSpec((1, LANES), lambda b, p, i: (0, 0))],
        out_specs=pl.BlockSpec((tq, LANES), lambda b, p, i: (b * nq + i, p)),
        out_shape=jax.ShapeDtypeStruct((n, SB_WIDTH), BF16),
        compiler_params=_params("parallel", "parallel", "arbitrary"),
        name="sb_attention",
    )(proj, proj, proj, sb_norm_w2)


def _out_proj_kernel(dn_ref, sb_ref, x_ref, w_ref, nw_ref, o_ref):
    half = dn_ref.shape[1]
    mix = _dot(dn_ref[...], w_ref[0:half, :]) + _dot(sb_ref[...], w_ref[half:2 * half, :])
    o_ref[...] = x_ref[...] + _rms_rows(mix, nw_ref[...])


def _out_proj(dn, sb, x2, w_out, nw, tm=512):
    n, d = x2.shape
    half = dn.shape[1]
    return pl.pallas_call(
        _out_proj_kernel,
        grid=(n // tm,),
        in_specs=[pl.BlockSpec((tm, half), lambda i: (i, 0)),
                  pl.BlockSpec((tm, half), lambda i: (i, 0)),
                  pl.BlockSpec((tm, d), lambda i: (i, 0)),
                  pl.BlockSpec((2 * half, d), lambda i: (0, 0)),
                  pl.BlockSpec((1, d), lambda i: (0, 0))],
        out_specs=pl.BlockSpec((tm, d), lambda i: (i, 0)),
        out_shape=jax.ShapeDtypeStruct((n, d), F32),
        compiler_params=_params("parallel"),
        name="out_proj",
    )(dn, sb, x2, w_out, nw)


def _dense_ffn_kernel(x_ref, pre_ref, post_ref, wg_ref, wu_ref, wd_ref, o_ref, h_ref, acc_ref):
    j = pl.program_id(1)

    @pl.when(j == 0)
    def _():
        h_ref[...] = _rms_rows(x_ref[...], pre_ref[...]).astype(BF16)

    h = h_ref[...]
    g = _dot(h, wg_ref[...])
    u = _dot(h, wu_ref[...])
    d = _dot((g * _sigmoid(g) * u).astype(BF16), wd_ref[...])

    @pl.when(j == 0)
    def _():
        acc_ref[...] = d

    @pl.when(j > 0)
    def _():
        acc_ref[...] += d

    @pl.when(j == pl.num_programs(1) - 1)
    def _():
        o_ref[...] = x_ref[...] + _rms_rows(acc_ref[...], post_ref[...])


def _dense_ffn(x2, pre, post, wg, wu, wd, tm=512, tf=1408):
    n, d = x2.shape
    ff = wg.shape[1]
    return pl.pallas_call(
        _dense_ffn_kernel,
        grid=(n // tm, ff // tf),
        in_specs=[pl.BlockSpec((tm, d), lambda i, j: (i, 0)),
                  pl.BlockSpec((1, d), lambda i, j: (0, 0)),
                  pl.BlockSpec((1, d), lambda i, j: (0, 0)),
                  pl.BlockSpec((d, tf), lambda i, j: (0, j)),
                  pl.BlockSpec((d, tf), lambda i, j: (0, j)),
                  pl.BlockSpec((tf, d), lambda i, j: (j, 0))],
        out_specs=pl.BlockSpec((tm, d), lambda i, j: (i, 0)),
        out_shape=jax.ShapeDtypeStruct((n, d), F32),
        scratch_shapes=[pltpu.VMEM((tm, d), BF16), pltpu.VMEM((tm, d), F32)],
        compiler_params=_params("parallel", "arbitrary"),
        name="dense_ffn",
    )(x2, pre, post, wg, wu, wd)


def _router_kernel(x_ref, nw_ref, rhi_ref, rlo_ref, h_ref, meta_ref, cnt_ref, run_ref):
    tm = x_ref.shape[0]

    @pl.when(pl.program_id(0) == 0)
    def _():
        run_ref[...] = jnp.zeros_like(run_ref)

    hf = _rms_rows(x_ref[...], nw_ref[...])
    h_ref[...] = hf
    hb = hf.astype(BF16)
    hlo = (hf - hb.astype(F32)).astype(BF16)
    logits = _dot(hb, rhi_ref[...]) + _dot(hb, rlo_ref[...]) + _dot(hlo, rhi_ref[...])

    lane = lax.broadcasted_iota(jnp.int32, (tm, LANES), 1).astype(F32)
    lg = jnp.where(lane < N_EXPERTS, logits, -jnp.inf)
    m1 = jnp.max(lg, axis=-1, keepdims=True)
    i1 = jnp.min(jnp.where(lg == m1, lane, float(LANES)), axis=-1, keepdims=True)
    lg2 = jnp.where(lane == i1, -jnp.inf, lg)
    m2 = jnp.max(lg2, axis=-1, keepdims=True)
    i2 = jnp.min(jnp.where(lg2 == m2, lane, float(LANES)), axis=-1, keepdims=True)
    e = jnp.exp(m2 - m1)
    g1 = 1.0 / (1.0 + e)
    g2 = e * g1

    onehot = jnp.where(lane == i1, 1.0, jnp.where(lane == i2, 1.0, 0.0))
    r = lax.broadcasted_iota(jnp.int32, (tm, tm), 0)
    c = lax.broadcasted_iota(jnp.int32, (tm, tm), 1)
    before = jnp.where(c < r, 1.0, 0.0).astype(BF16)
    rank = _dot(before, onehot.astype(BF16)) + run_ref[...]
    r1 = jnp.sum(jnp.where(lane == i1, rank, 0.0), axis=-1, keepdims=True)
    r2 = jnp.sum(jnp.where(lane == i2, rank, 0.0), axis=-1, keepdims=True)
    run_ref[...] += jnp.sum(onehot, axis=0, keepdims=True)
    cnt_ref[...] = run_ref[...]

    meta = jnp.zeros((tm, LANES), F32)
    for k, val in enumerate((i1, i2, g1, g2, r1, r2)):
        meta = jnp.where(lane == float(k), val, meta)
    meta_ref[...] = meta


def _router(x2, nw, r_hi, r_lo, tm=512):
    n, d = x2.shape
    return pl.pallas_call(
        _router_kernel,
        grid=(n // tm,),
        in_specs=[pl.BlockSpec((tm, d), lambda i: (i, 0)),
                  pl.BlockSpec((1, d), lambda i: (0, 0)),
                  pl.BlockSpec((d, LANES), lambda i: (0, 0)),
                  pl.BlockSpec((d, LANES), lambda i: (0, 0))],
        out_specs=[pl.BlockSpec((tm, d), lambda i: (i, 0)),
                   pl.BlockSpec((tm, LANES), lambda i: (i, 0)),
                   pl.BlockSpec((1, LANES), lambda i: (0, 0))],
        out_shape=[jax.ShapeDtypeStruct((n, d), F32),
                   jax.ShapeDtypeStruct((n, LANES), F32),
                   jax.ShapeDtypeStruct((1, LANES), F32)],
        scratch_shapes=[pltpu.VMEM((1, LANES), F32)],
        compiler_params=_params("arbitrary"),
        name="router",
    )(x2, nw, r_hi, r_lo)


def _row_copy(src_ref, src_row, dst_ref, dst_row, sem):
    return pltpu.make_async_copy(src_ref.at[pl.ds(src_row, 1)], dst_ref.at[pl.ds(dst_row, 1)], sem)


def _dispatch_kernel(pos_ref, h_ref, xs_in_ref, xs_ref, sem, *, tb):
    del xs_in_ref

    def copies(t):
        return (_row_copy(h_ref, t, xs_ref, pos_ref[2 * t], sem),
                _row_copy(h_ref, t, xs_ref, pos_ref[2 * t + 1], sem))

    def issue(g, carry):
        for u in range(ROW_DMA_UNROLL):
            for cp in copies(g * ROW_DMA_UNROLL + u):
                cp.start()
        return carry

    lax.fori_loop(0, tb // ROW_DMA_UNROLL, issue, 0)

    def drain(g, carry):
        for u in range(ROW_DMA_UNROLL):
            for cp in copies(g * ROW_DMA_UNROLL + u):
                cp.wait()
        return carry

    lax.fori_loop(0, tb // ROW_DMA_UNROLL, drain, 0)


def _dispatch(pos_flat, h, xs_zero, tb=512):
    n, d = h.shape
    return pl.pallas_call(
        functools.partial(_dispatch_kernel, tb=tb),
        grid=(n // tb,),
        in_specs=[pl.BlockSpec((2 * tb,), lambda i: (i,), memory_space=pltpu.SMEM),
                  pl.BlockSpec((tb, d), lambda i: (i, 0)),
                  pl.BlockSpec(memory_space=pl.ANY)],
        out_specs=pl.BlockSpec(memory_space=pl.ANY),
        out_shape=jax.ShapeDtypeStruct(xs_zero.shape, xs_zero.dtype),
        scratch_shapes=[pltpu.SemaphoreType.DMA(())],
        input_output_aliases={2: 0},
        compiler_params=_params("arbitrary"),
        name="moe_dispatch",
    )(pos_flat, h, xs_zero)


def _moe_ffn_kernel(te_ref, nv_ref, xs_ref, wg_ref, wu_ref, wd_ref, ys_ref, h_ref, acc_ref):
    del te_ref
    i = pl.program_id(0)
    j = pl.program_id(1)
    last = pl.num_programs(1) - 1
    valid = i < nv_ref[0]

    @pl.when(valid)
    def _():
        @pl.when(j == 0)
        def _():
            h_ref[...] = xs_ref[...].astype(BF16)

        h = h_ref[...]
        g = _dot(h, wg_ref[0])
        u = _dot(h, wu_ref[0])
        d = _dot((g * _sigmoid(g) * u).astype(BF16), wd_ref[0])

        @pl.when(j == 0)
        def _():
            acc_ref[...] = d

        @pl.when(j > 0)
        def _():
            acc_ref[...] += d

        @pl.when(j == last)
        def _():
            ys_ref[...] = acc_ref[...]

    @pl.when(jnp.logical_and(jnp.logical_not(valid), j == last))
    def _():
        ys_ref[...] = jnp.zeros_like(ys_ref)


def _moe_ffn(tile_expert, n_valid, xs, wg, wu, wd, tm, tf):
    rows, d = xs.shape
    ff = wg.shape[2]
    grid_spec = pltpu.PrefetchScalarGridSpec(
        num_scalar_prefetch=2,
        grid=(rows // tm, ff // tf),
        in_specs=[pl.BlockSpec((tm, d), lambda i, j, te, nv: (i, 0)),
                  pl.BlockSpec((1, d, tf), lambda i, j, te, nv: (te[i], 0, j)),
                  pl.BlockSpec((1, d, tf), lambda i, j, te, nv: (te[i], 0, j)),
                  pl.BlockSpec((1, tf, d), lambda i, j, te, nv: (te[i], j, 0))],
        out_specs=pl.BlockSpec((tm, d), lambda i, j, te, nv: (i, 0)),
        scratch_shapes=[pltpu.VMEM((tm, d), BF16), pltpu.VMEM((tm, d), F32)],
    )
    return pl.pallas_call(
        _moe_ffn_kernel,
        grid_spec=grid_spec,
        out_shape=jax.ShapeDtypeStruct((rows, d), F32),
        compiler_params=_params("arbitrary", "arbitrary"),
        name="moe_ffn",
    )(tile_expert, n_valid, xs, wg, wu, wd)


def _combine_kernel(pos_ref, ys_ref, x_ref, meta_ref, nw_ref, o_ref, y0_ref, y1_ref, sem, *, tb):
    def copies(t):
        return (_row_copy(ys_ref, pos_ref[2 * t], y0_ref, t, sem),
                _row_copy(ys_ref, pos_ref[2 * t + 1], y1_ref, t, sem))

    def issue(g, carry):
        for u in range(ROW_DMA_UNROLL):
            for cp in copies(g * ROW_DMA_UNROLL + u):
                cp.start()
        return carry

    lax.fori_loop(0, tb // ROW_DMA_UNROLL, issue, 0)

    def drain(g, carry):
        for u in range(ROW_DMA_UNROLL):
            for cp in copies(g * ROW_DMA_UNROLL + u):
                cp.wait()
        return carry

    lax.fori_loop(0, tb // ROW_DMA_UNROLL, drain, 0)

    meta = meta_ref[...]
    y = meta[:, 2:3] * y0_ref[...] + meta[:, 3:4] * y1_ref[...]
    o_ref[...] = x_ref[...] + _rms_rows(y, nw_ref[...])


def _combine(pos_flat, ys, x2, meta, nw, tb=512):
    n, d = x2.shape
    return pl.pallas_call(
        functools.partial(_combine_kernel, tb=tb),
        grid=(n // tb,),
        in_specs=[pl.BlockSpec((2 * tb,), lambda i: (i,), memory_space=pltpu.SMEM),
                  pl.BlockSpec(memory_space=pl.ANY),
                  pl.BlockSpec((tb, d), lambda i: (i, 0)),
                  pl.BlockSpec((tb, LANES), lambda i: (i, 0)),
                  pl.BlockSpec((1, d), lambda i: (0, 0))],
        out_specs=pl.BlockSpec((tb, d), lambda i: (i, 0)),
        out_shape=jax.ShapeDtypeStruct((n, d), F32),
        scratch_shapes=[pltpu.VMEM((tb, d), F32), pltpu.VMEM((tb, d), F32),
                        pltpu.SemaphoreType.DMA(())],
        compiler_params=_params("arbitrary"),
        name="moe_combine",
    )(pos_flat, ys, x2, meta, nw)


def _moe_layer(x2, pre, post, router_w, wg, wu, wd, tm=512, tf=1792):
    n, d = x2.shape
    r_pad = jnp.pad(router_w, ((0, 0), (0, LANES - N_EXPERTS)))
    r_hi = r_pad.astype(BF16)
    r_lo = (r_pad - r_hi.astype(F32)).astype(BF16)
    h, meta, cnt = _router(x2, pre, r_hi, r_lo)

    idx = meta[:, 0:2].astype(jnp.int32)
    rank = meta[:, 4:6].astype(jnp.int32)
    counts = cnt[0, 0:N_EXPERTS].astype(jnp.int32)
    padded = ((counts + tm - 1) // tm) * tm
    ends = jnp.cumsum(padded)
    offsets = ends - padded
    pos_flat = (offsets[idx] + rank).reshape(-1)
    n_tiles = (2 * n) // tm + N_EXPERTS
    tile_start = jnp.arange(n_tiles, dtype=jnp.int32) * tm
    tile_expert = jnp.minimum(
        jnp.sum((tile_start[:, None] >= ends[None, :]).astype(jnp.int32), axis=1), N_EXPERTS - 1)
    n_valid = (ends[-1:] // tm).astype(jnp.int32)

    xs = _dispatch(pos_flat, h, jnp.zeros((n_tiles * tm, d), F32))
    ys = _moe_ffn(tile_expert, n_valid, xs, wg.astype(BF16), wu.astype(BF16), wd.astype(BF16),
                  tm, tf)
    return _combine(pos_flat, ys, x2, meta, post)


def kernel(x, norm_mix_pre, norm_mix_post, norm_ffn_pre, norm_ffn_post, w_in, conv_w, dn_a_log,
           dn_dt_bias, dn_norm_w, sb_norm_w, w_out, ffn_w_gate, ffn_w_up, ffn_w_down, router_w,
           moe_w_gate, moe_w_up, moe_w_down):
    batch, seq, d = x.shape
    depth = w_in.shape[0]
    x2 = x.reshape(batch * seq, d)
    gate_cols = 4 * DN_WIDTH
    n_small = 2 * DN_HEADS
    sb_col0 = gate_cols // LANES

    def pad_lanes(v, lane0):
        return jnp.zeros((1, LANES), F32).at[0, lane0:lane0 + v.shape[0]].set(v)

    for layer in range(depth):
        w = w_in[layer]
        w_big = jnp.concatenate([w[:, :gate_cols], w[:, gate_cols + n_small:]], axis=1).astype(BF16)
        w_small = jnp.pad(w[:, gate_cols:gate_cols + n_small],
                          ((0, 0), (0, LANES - n_small))).astype(BF16)
        proj, small = _in_proj(x2, norm_mix_pre[layer][None, :], w_big, w_small)
        dn = _deltanet(proj, small, conv_w[layer], pad_lanes(dn_a_log[layer], DN_HEADS),
                       pad_lanes(dn_dt_bias[layer], DN_HEADS), dn_norm_w[layer][None, :],
                       batch, seq)
        sb = _sb_attention(proj, jnp.tile(sb_norm_w[layer], 2)[None, :], batch, seq, sb_col0)
        x2 = _out_proj(dn, sb, x2, w_out[layer].astype(BF16), norm_mix_post[layer][None, :])

        pre = norm_ffn_pre[layer][None, :]
        post = norm_ffn_post[layer][None, :]
        i = layer // 2
        if layer % 2 == 0:
            x2 = _dense_ffn(x2, pre, post, ffn_w_gate[i].astype(BF16), ffn_w_up[i].astype(BF16),
                            ffn_w_down[i].astype(BF16))
        else:
            x2 = _moe_layer(x2, pre, post, router_w[i], moe_w_gate[i], moe_w_up[i], moe_w_down[i])
    return x2.reshape(batch, seq, d)
```

```python
import functools

import jax
import jax.numpy as jnp
from jax import lax
from jax.experimental import pallas as pl
from jax.experimental.pallas import tpu as pltpu

F32 = jnp.float32
BF16 = jnp.bfloat16
EPS = 1e-6
LOG2E = 1.4426950408889634
MASKED_SCORE = -1e30

LANES = 128
DN_HEADS = 4
DN_HEAD_DIM = 128
DN_WIDTH = DN_HEADS * DN_HEAD_DIM
SB_HEAD_DIM = 64
SB_WIDTH = 512
SB_PAIRS = SB_WIDTH // LANES
CONV_K = 4
CHUNK = 64
N_EXPERTS = 8
ROW_DMA_UNROLL = 8
VMEM_LIMIT = 56 * 1024 * 1024


def _dot(a, b):
    return jnp.dot(a, b, preferred_element_type=F32)


def _dot_nt(a, b):
    return lax.dot_general(a, b, (((1,), (1,)), ((), ())), preferred_element_type=F32)


def _sigmoid(x):
    return 1.0 / (1.0 + jnp.exp(-x))


def _softplus(x):
    return jnp.maximum(x, 0.0) + jnp.log1p(jnp.exp(-jnp.abs(x)))


def _rms_rows(x, w):
    ms = jnp.mean(x * x, axis=-1, keepdims=True)
    return x * lax.rsqrt(ms + EPS) * w


def _split3(x):
    hi = x.astype(BF16)
    r1 = x - hi.astype(F32)
    mid = r1.astype(BF16)
    lo = (r1 - mid.astype(F32)).astype(BF16)
    return hi, mid, lo


def _params(*sem):
    return pltpu.CompilerParams(dimension_semantics=sem, vmem_limit_bytes=VMEM_LIMIT)


def _in_proj_kernel(x_ref, nw_ref, w_ref, ws_ref, cw_ref, o_ref, os_ref, hist_ref,
                    *, col_chunk, conv_chunks, tiles_per_seq):
    tm = x_ref.shape[0]

    @pl.when(pl.program_id(0) % tiles_per_seq == 0)
    def _():
        hist_ref[0:8, :] = jnp.zeros((8, hist_ref.shape[1]), F32)

    h = _rms_rows(x_ref[...], nw_ref[...]).astype(BF16)
    for c in range(w_ref.shape[1] // col_chunk):
        cols = slice(c * col_chunk, (c + 1) * col_chunk)
        r = _dot(h, w_ref[:, cols])
        if c >= conv_chunks:
            o_ref[:, cols] = r.astype(BF16)
            continue
        hist_ref[8:8 + tm, cols] = r
        acc = r * cw_ref[CONV_K - 1:CONV_K, cols]
        for i in range(CONV_K - 1):
            acc = acc + hist_ref[5 + i:5 + i + tm, cols] * cw_ref[i:i + 1, cols]
        hist_ref[0:8, cols] = hist_ref[tm:tm + 8, cols]
        o_ref[:, cols] = (acc * _sigmoid(acc)).astype(BF16)
    os_ref[...] = _dot(h, ws_ref[...])


def _in_proj(x2, nw, w_big, w_small, conv_w, seq, tm=512):
    n, d = x2.shape
    p = w_big.shape[1]
    conv_cols = conv_w.shape[1]
    return pl.pallas_call(
        functools.partial(_in_proj_kernel, col_chunk=512, conv_chunks=conv_cols // 512,
                          tiles_per_seq=seq // tm),
        grid=(n // tm,),
        in_specs=[pl.BlockSpec((tm, d), lambda i: (i, 0)),
                  pl.BlockSpec((1, d), lambda i: (0, 0)),
                  pl.BlockSpec((d, p), lambda i: (0, 0)),
                  pl.BlockSpec((d, LANES), lambda i: (0, 0)),
                  pl.BlockSpec((CONV_K, conv_cols), lambda i: (0, 0))],
        out_specs=[pl.BlockSpec((tm, p), lambda i: (i, 0)),
                   pl.BlockSpec((tm, LANES), lambda i: (i, 0))],
        out_shape=[jax.ShapeDtypeStruct((n, p), BF16),
                   jax.ShapeDtypeStruct((n, LANES), F32)],
        scratch_shapes=[pltpu.VMEM((tm + 8, conv_cols), F32)],
        compiler_params=_params("arbitrary"),
        name="in_proj",
    )(x2, nw, w_big, w_small, conv_w)


def _deltanet_kernel(q_ref, k_ref, v_ref, z_ref, s_ref, alog_ref, dtb_ref, nw_ref,
                     o_ref, state_ref, *, tc):
    hd = DN_HEAD_DIM
    st = DN_HEADS * CHUNK

    @pl.when(pl.program_id(1) == 0)
    def _init():
        state_ref[...] = jnp.zeros_like(state_ref)

    def l2n(a):
        return a * lax.rsqrt(jnp.sum(a * a, axis=-1, keepdims=True) + EPS)

    def head(ref, h):
        return ref[:, h * hd:(h + 1) * hd].astype(F32)

    qn = [l2n(head(q_ref, h)) * (hd ** -0.5) for h in range(DN_HEADS)]
    kn = [l2n(head(k_ref, h)) for h in range(DN_HEADS)]
    vv = [head(v_ref, h) for h in range(DN_HEADS)]

    sm = s_ref[...]
    beta_t = _sigmoid(sm)
    g_t = -jnp.exp(alog_ref[...]) * _softplus(sm + dtb_ref[...])

    r = lax.broadcasted_iota(jnp.int32, (tc, tc), 0)
    c = lax.broadcasted_iota(jnp.int32, (tc, tc), 1)
    same = (r // CHUNK) == (c // CHUNK)
    cum_blk = jnp.where(same, jnp.where(c <= r, 1.0, 0.0), 0.0).astype(BF16)
    tot_blk = jnp.where(same, 1.0, 0.0).astype(BF16)
    g3 = jnp.concatenate(_split3(g_t), axis=1)
    red = _dot(jnp.concatenate([cum_blk, tot_blk], axis=0), g3)
    red = red[:, 0:LANES] + red[:, LANES:2 * LANES] + red[:, 2 * LANES:3 * LANES]
    gc_t, gl_t = red[0:tc], red[tc:2 * tc]

    rs = lax.broadcasted_iota(jnp.int32, (st, st), 0)
    cs = lax.broadcasted_iota(jnp.int32, (st, st), 1)
    same_s = (rs // CHUNK) == (cs // CHUNK)
    mask_incl = jnp.logical_and(same_s, cs <= rs)
    mask_strict = jnp.logical_and(same_s, cs < rs)
    eye = jnp.where(rs == cs, 1.0, 0.0).astype(F32)
    nw = nw_ref[...]

    n_chunks = tc // CHUNK
    head_of_row = lax.broadcasted_iota(jnp.int32, (st, hd), 0) // CHUNK

    pre = []
    for ci in range(n_chunks):
        rows = slice(ci * CHUNK, (ci + 1) * CHUNK)

        def stack(parts):
            return jnp.concatenate([p[rows] for p in parts], axis=0)

        def stack_col(a, lane0):
            return jnp.concatenate(
                [jnp.broadcast_to(a[rows, lane0 + h:lane0 + h + 1], (CHUNK, hd))
                 for h in range(DN_HEADS)], axis=0)

        qs, ks, vs = stack(qn), stack(kn), stack(vv)
        bs = stack_col(beta_t, 0)
        gc = stack_col(gc_t, DN_HEADS)
        gl = stack_col(gl_t, DN_HEADS)
        kb = ks * bs
        vb = vs * bs
        eg = jnp.exp(gc)
        kbe = kb * eg

        gct = gc.T
        diff = jnp.concatenate([gc, gc], axis=1) - jnp.concatenate([gct, gct], axis=0)
        dm = jnp.exp(jnp.where(mask_incl, diff, -jnp.inf))

        m1 = _dot_nt(jnp.concatenate([kb, qs], axis=0).astype(BF16), ks.astype(BF16))
        lmat = jnp.where(mask_strict, m1[0:st] * dm, 0.0)
        pre.append(dict(
            rows=rows,
            pmat=eye - lmat,
            xb=lmat.astype(BF16),
            rhs=jnp.concatenate([vb, kbe], axis=1).astype(BF16),
            attn=(m1[st:2 * st] * dm).astype(BF16),
            qd_b=(qs * eg).astype(BF16),
            kend_t=(ks * jnp.exp(gl - gc)).T.astype(BF16),
            egl=jnp.exp(gl)))

    for _ in range(5):
        for c in pre:
            c["xb"] = _dot(c["xb"], c["xb"]).astype(BF16)
        for c in pre:
            c["pmat"] = c["pmat"] + _dot(c["pmat"].astype(BF16), c["xb"])
    for c in pre:
        c["uw"] = _dot(c["pmat"].astype(BF16), c["rhs"])

    for c in pre:
        rows, uw, attn, qd_b, egl = c["rows"], c["uw"], c["attn"], c["qd_b"], c["egl"]
        u, wmat = uw[:, 0:hd], uw[:, hd:2 * hd].astype(BF16)

        s_cat = state_ref[...]
        s_b = s_cat.astype(BF16)
        vn = jnp.concatenate(
            [u[h * CHUNK:(h + 1) * CHUNK]
             - _dot(wmat[h * CHUNK:(h + 1) * CHUNK], s_b[:, h * hd:(h + 1) * hd])
             for h in range(DN_HEADS)], axis=0)
        vn_b = vn.astype(BF16)
        o_intra = _dot(attn, vn_b)

        vn_bd = jnp.concatenate(
            [jnp.where(head_of_row == h, vn_b, jnp.zeros_like(vn_b)) for h in range(DN_HEADS)],
            axis=1)
        decay = jnp.concatenate(
            [jnp.concatenate([egl[h * CHUNK:(h + 1) * CHUNK]] * (hd // CHUNK), axis=0)
             for h in range(DN_HEADS)], axis=1)
        state_ref[...] = s_cat * decay + _dot(c["kend_t"], vn_bd)

        for h in range(DN_HEADS):
            hs = slice(h * CHUNK, (h + 1) * CHUNK)
            o_h = _dot(qd_b[hs], s_b[:, h * hd:(h + 1) * hd]) + o_intra[hs]
            zh = z_ref[rows, h * hd:(h + 1) * hd].astype(F32)
            o_ref[rows, h * hd:(h + 1) * hd] = (
                _rms_rows(o_h, nw) * (zh * _sigmoid(zh))).astype(BF16)


def _deltanet(proj, small, alog_pad, dtb_pad, dn_norm_w, batch, seq, tc=512):
    n = proj.shape[0]
    nt = seq // tc
    row = lambda b, t: b * nt + t
    return pl.pallas_call(
        functools.partial(_deltanet_kernel, tc=tc),
        grid=(batch, nt),
        in_specs=[pl.BlockSpec((tc, DN_WIDTH), lambda b, t: (row(b, t), 0)),
                  pl.BlockSpec((tc, DN_WIDTH), lambda b, t: (row(b, t), 1)),
                  pl.BlockSpec((tc, DN_WIDTH), lambda b, t: (row(b, t), 2)),
                  pl.BlockSpec((tc, DN_WIDTH), lambda b, t: (row(b, t), 3)),
                  pl.BlockSpec((tc, LANES), lambda b, t: (row(b, t), 0)),
                  pl.BlockSpec((1, LANES), lambda b, t: (0, 0)),
                  pl.BlockSpec((1, LANES), lambda b, t: (0, 0)),
                  pl.BlockSpec((1, DN_HEAD_DIM), lambda b, t: (0, 0))],
        out_specs=pl.BlockSpec((tc, DN_WIDTH), lambda b, t: (row(b, t), 0)),
        out_shape=jax.ShapeDtypeStruct((n, DN_WIDTH), BF16),
        scratch_shapes=[pltpu.VMEM((DN_HEAD_DIM, DN_WIDTH), F32)],
        compiler_params=_params("parallel", "arbitrary"),
        name="deltanet",
    )(proj, proj, proj, proj, small, alog_pad, dtb_pad, dn_norm_w)


def _sb_kernel(q_ref, k_ref, v_ref, nw_ref, o_ref, qst_ref, total_ref, acc_ref, *, tq, nq):
    tk = tq
    i = pl.program_id(2)
    lane = lax.broadcasted_iota(jnp.int32, (1, LANES), 1)
    first = lane < SB_HEAD_DIM
    last_lane = lane == LANES - 1
    blocks = (i, nq - 1 - i)

    for s in range(2):
        q2 = q_ref[pl.ds(pl.multiple_of(blocks[s] * tq, tq), tq), :] * (SB_HEAD_DIM ** -0.5)
        zero = jnp.zeros_like(q2)
        qst_ref[s] = jnp.concatenate([jnp.where(first, q2, zero), jnp.where(first, zero, q2)],
                                     axis=0)
    total_ref[...] = jnp.zeros_like(total_ref)
    acc_ref[...] = jnp.zeros_like(acc_ref)

    rq = lax.broadcasted_iota(jnp.int32, (2 * tq, tk), 0)
    ck = lax.broadcasted_iota(jnp.int32, (2 * tq, tk), 1)
    diag_bias = jnp.where(ck < jnp.where(rq >= tq, rq - tq, rq), 0.0, MASKED_SCORE)

    rj = lax.broadcasted_iota(jnp.int32, (tk, tk), 0)
    cs = lax.broadcasted_iota(jnp.int32, (tk, tk), 1)
    suffix = jnp.where(rj >= cs, 1.0, 0.0).astype(BF16)

    def visit(v):
        if v < 2:
            slot, tile = v, blocks[v]
        else:
            second = v - 2 >= i
            slot = second.astype(jnp.int32)
            tile = jnp.where(second, nq - v, i + 1 - v)
        return slot, pl.multiple_of(tile * tk, tk)

    def scores(v):
        slot, start = visit(v)
        z = _dot_nt(qst_ref[slot], k_ref[pl.ds(start, tk), :])
        return z + diag_bias if v < 2 else z

    def weights(v, z):
        slot, _ = visit(v)
        drop = jnp.maximum(z, 0.0) + jnp.log(1.0 + jnp.exp2(jnp.abs(z) * -LOG2E))
        parked = jnp.where(last_lane, pltpu.roll(total_ref[slot], LANES - 1, 1), 0.0)
        drop = jnp.concatenate([drop[:, 0:tk - LANES], drop[:, tk - LANES:tk] + parked], axis=1)
        inc = _dot(drop.astype(BF16), suffix)
        total_ref[slot] = inc[:, 0:LANES]
        return jnp.exp2((z - inc) * LOG2E).astype(BF16)

    def output(v, w):
        slot, start = visit(v)
        acc_ref[slot] += _dot(w, v_ref[pl.ds(start, tk), :])

    n_visits = nq + 1
    z, w = {}, {}
    for n in range(n_visits + 2):
        if n < n_visits:
            z[n] = scores(n)
        if 1 <= n <= n_visits:
            w[n - 1] = weights(n - 1, z.pop(n - 1))
        if n >= 2:
            output(n - 2, w.pop(n - 2))

    for s in range(2):
        acc = acc_ref[s]
        o2 = jnp.where(first, acc[0:tq], acc[tq:2 * tq])
        sq = o2 * o2
        s_first = jnp.sum(jnp.where(first, sq, 0.0), axis=-1, keepdims=True)
        s_all = jnp.sum(sq, axis=-1, keepdims=True)
        ms = jnp.where(first, s_first, s_all - s_first) * (1.0 / SB_HEAD_DIM)
        o_ref[pl.ds(pl.multiple_of(blocks[s] * tq, tq), tq), :] = (
            o2 * lax.rsqrt(ms + EPS) * nw_ref[...]).astype(BF16)


def _sb_attention(proj, sb_norm_w2, batch, seq, col0, tq=256):
    n = proj.shape[0]
    nq = seq // tq
    assert nq % 2 == 0
    seq_block = lambda c: pl.BlockSpec((seq, LANES), lambda b, p, i: (b, c + p))
    return pl.pallas_call(
        functools.partial(_sb_kernel, tq=tq, nq=nq),
        grid=(batch, SB_PAIRS, nq // 2),
        in_specs=[seq_block(col0), seq_block(col0 + SB_PAIRS), seq_block(col0 + 2 * SB_PAIRS),
                  pl.BlockSpec((1, LANES), lambda b, p, i: (0, 0))],
        out_specs=pl.BlockSpec((seq, LANES), lambda b, p, i: (b, p)),
        out_shape=jax.ShapeDtypeStruct((n, SB_WIDTH), BF16),
        scratch_shapes=[pltpu.VMEM((2, 2 * tq, LANES), BF16),
                        pltpu.VMEM((2, 2 * tq, LANES), F32),
                        pltpu.VMEM((2, 2 * tq, LANES), F32)],
        compiler_params=_params("parallel", "parallel", "arbitrary"),
        name="sb_attention",
    )(proj, proj, proj, sb_norm_w2)


def _out_proj_kernel(dn_ref, sb_ref, x_ref, w_ref, nw_ref, o_ref):
    half = dn_ref.shape[1]
    mix = _dot(dn_ref[...], w_ref[0:half, :]) + _dot(sb_ref[...], w_ref[half:2 * half, :])
    o_ref[...] = x_ref[...] + _rms_rows(mix, nw_ref[...])


def _out_proj(dn, sb, x2, w_out, nw, tm=512):
    n, d = x2.shape
    half = dn.shape[1]
    return pl.pallas_call(
        _out_proj_kernel,
        grid=(n // tm,),
        in_specs=[pl.BlockSpec((tm, half), lambda i: (i, 0)),
                  pl.BlockSpec((tm, half), lambda i: (i, 0)),
                  pl.BlockSpec((tm, d), lambda i: (i, 0)),
                  pl.BlockSpec((2 * half, d), lambda i: (0, 0)),
                  pl.BlockSpec((1, d), lambda i: (0, 0))],
        out_specs=pl.BlockSpec((tm, d), lambda i: (i, 0)),
        out_shape=jax.ShapeDtypeStruct((n, d), F32),
        compiler_params=_params("parallel"),
        name="out_proj",
    )(dn, sb, x2, w_out, nw)


def _dense_ffn_kernel(x_ref, pre_ref, post_ref, wg_ref, wu_ref, wd_ref, o_ref, h_ref, acc_ref):
    j = pl.program_id(1)

    @pl.when(j == 0)
    def _():
        h_ref[...] = _rms_rows(x_ref[...], pre_ref[...]).astype(BF16)

    h = h_ref[...]
    g = _dot(h, wg_ref[...])
    u = _dot(h, wu_ref[...])
    d = _dot((g * _sigmoid(g) * u).astype(BF16), wd_ref[...])

    @pl.when(j == 0)
    def _():
        acc_ref[...] = d

    @pl.when(j > 0)
    def _():
        acc_ref[...] += d

    @pl.when(j == pl.num_programs(1) - 1)
    def _():
        o_ref[...] = x_ref[...] + _rms_rows(acc_ref[...], post_ref[...])


def _dense_ffn(x2, pre, post, wg, wu, wd, tm=512, tf=1408):
    n, d = x2.shape
    ff = wg.shape[1]
    return pl.pallas_call(
        _dense_ffn_kernel,
        grid=(n // tm, ff // tf),
        in_specs=[pl.BlockSpec((tm, d), lambda i, j: (i, 0)),
                  pl.BlockSpec((1, d), lambda i, j: (0, 0)),
                  pl.BlockSpec((1, d), lambda i, j: (0, 0)),
                  pl.BlockSpec((d, tf), lambda i, j: (0, j)),
                  pl.BlockSpec((d, tf), lambda i, j: (0, j)),
                  pl.BlockSpec((tf, d), lambda i, j: (j, 0))],
        out_specs=pl.BlockSpec((tm, d), lambda i, j: (i, 0)),
        out_shape=jax.ShapeDtypeStruct((n, d), F32),
        scratch_shapes=[pltpu.VMEM((tm, d), BF16), pltpu.VMEM((tm, d), F32)],
        compiler_params=_params("parallel", "arbitrary"),
        name="dense_ffn",
    )(x2, pre, post, wg, wu, wd)


def _router_kernel(x_ref, nw_ref, rhi_ref, rlo_ref, h_ref, meta_ref, cnt_ref, run_ref):
    tm = x_ref.shape[0]

    @pl.when(pl.program_id(0) == 0)
    def _():
        run_ref[...] = jnp.zeros_like(run_ref)

    hf = _rms_rows(x_ref[...], nw_ref[...])
    h_ref[...] = hf
    hb = hf.astype(BF16)
    hlo = (hf - hb.astype(F32)).astype(BF16)
    logits = _dot(hb, rhi_ref[...]) + _dot(hb, rlo_ref[...]) + _dot(hlo, rhi_ref[...])

    lane = lax.broadcasted_iota(jnp.int32, (tm, LANES), 1).astype(F32)
    lg = jnp.where(lane < N_EXPERTS, logits, -jnp.inf)
    m1 = jnp.max(lg, axis=-1, keepdims=True)
    i1 = jnp.min(jnp.where(lg == m1, lane, float(LANES)), axis=-1, keepdims=True)
    lg2 = jnp.where(lane == i1, -jnp.inf, lg)
    m2 = jnp.max(lg2, axis=-1, keepdims=True)
    i2 = jnp.min(jnp.where(lg2 == m2, lane, float(LANES)), axis=-1, keepdims=True)
    e = jnp.exp(m2 - m1)
    g1 = 1.0 / (1.0 + e)
    g2 = e * g1

    onehot = jnp.where(lane == i1, 1.0, jnp.where(lane == i2, 1.0, 0.0))
    r = lax.broadcasted_iota(jnp.int32, (tm, tm), 0)
    c = lax.broadcasted_iota(jnp.int32, (tm, tm), 1)
    before = jnp.where(c < r, 1.0, 0.0).astype(BF16)
    rank = _dot(before, onehot.astype(BF16)) + run_ref[...]
    r1 = jnp.sum(jnp.where(lane == i1, rank, 0.0), axis=-1, keepdims=True)
    r2 = jnp.sum(jnp.where(lane == i2, rank, 0.0), axis=-1, keepdims=True)
    run_ref[...] += jnp.sum(onehot, axis=0, keepdims=True)
    cnt_ref[...] = run_ref[...]

    meta = jnp.zeros((tm, LANES), F32)
    for k, val in enumerate((i1, i2, g1, g2, r1, r2)):
        meta = jnp.where(lane == float(k), val, meta)
    meta_ref[...] = meta


def _router(x2, nw, r_hi, r_lo, tm=512):
    n, d = x2.shape
    return pl.pallas_call(
        _router_kernel,
        grid=(n // tm,),
        in_specs=[pl.BlockSpec((tm, d), lambda i: (i, 0)),
                  pl.BlockSpec((1, d), lambda i: (0, 0)),
                  pl.BlockSpec((d, LANES), lambda i: (0, 0)),
                  pl.BlockSpec((d, LANES), lambda i: (0, 0))],
        out_specs=[pl.BlockSpec((tm, d), lambda i: (i, 0)),
                   pl.BlockSpec((tm, LANES), lambda i: (i, 0)),
                   pl.BlockSpec((1, LANES), lambda i: (0, 0))],
        out_shape=[jax.ShapeDtypeStruct((n, d), F32),
                   jax.ShapeDtypeStruct((n, LANES), F32),
                   jax.ShapeDtypeStruct((1, LANES), F32)],
        scratch_shapes=[pltpu.VMEM((1, LANES), F32)],
        compiler_params=_params("arbitrary"),
        name="router",
    )(x2, nw, r_hi, r_lo)


def _row_copy(src_ref, src_row, dst_ref, dst_row, sem):
    return pltpu.make_async_copy(src_ref.at[pl.ds(src_row, 1)], dst_ref.at[pl.ds(dst_row, 1)], sem)


def _dispatch_kernel(pos_ref, ends_ref, h_ref, xs_ref, zero_ref, sem, zero_sem, *, tb, tm):
    @pl.when(pl.program_id(0) == 0)
    def _():
        zero_ref[...] = jnp.zeros_like(zero_ref)

        def fill(e):
            if e < N_EXPERTS:
                start = ends_ref[e] - tm
                wanted = start >= (ends_ref[e - 1] if e else 0)
            else:
                start = ends_ref[N_EXPERTS - 1] + (e - N_EXPERTS) * tm
                wanted = start < xs_ref.shape[0]
            start = pl.multiple_of(jnp.clip(start, 0, xs_ref.shape[0] - tm), tm)
            return wanted, pltpu.make_async_copy(zero_ref, xs_ref.at[pl.ds(start, tm)], zero_sem)

        for e in range(2 * N_EXPERTS):
            wanted, cp = fill(e)
            pl.when(wanted)(cp.start)
        for e in range(2 * N_EXPERTS):
            wanted, cp = fill(e)
            pl.when(wanted)(cp.wait)

    def copies(t):
        return (_row_copy(h_ref, t, xs_ref, pos_ref[2 * t], sem),
                _row_copy(h_ref, t, xs_ref, pos_ref[2 * t + 1], sem))

    def issue(g, carry):
        for u in range(ROW_DMA_UNROLL):
            for k, cp in enumerate(copies(g * ROW_DMA_UNROLL + u)):
                cp.start(priority=k)
        return carry

    lax.fori_loop(0, tb // ROW_DMA_UNROLL, issue, 0)

    def drain(g, carry):
        for u in range(ROW_DMA_UNROLL):
            for cp in copies(g * ROW_DMA_UNROLL + u):
                cp.wait()
        return carry

    lax.fori_loop(0, tb // ROW_DMA_UNROLL, drain, 0)


def _dispatch(pos_flat, ends, h, rows, tm, tb=512):
    n, d = h.shape
    return pl.pallas_call(
        functools.partial(_dispatch_kernel, tb=tb, tm=tm),
        grid=(n // tb,),
        in_specs=[pl.BlockSpec((2 * tb,), lambda i: (i,), memory_space=pltpu.SMEM),
                  pl.BlockSpec(memory_space=pltpu.SMEM),
                  pl.BlockSpec((tb, d), lambda i: (i, 0))],
        out_specs=pl.BlockSpec(memory_space=pl.ANY),
        out_shape=jax.ShapeDtypeStruct((rows, d), h.dtype),
        scratch_shapes=[pltpu.VMEM((tm, d), h.dtype), pltpu.SemaphoreType.DMA(()),
                        pltpu.SemaphoreType.DMA(())],
        compiler_params=_params("arbitrary"),
        name="moe_dispatch",
    )(pos_flat, ends, h)


def _moe_ffn_kernel(te_ref, nv_ref, xs_ref, wg_ref, wu_ref, wd_ref, ys_ref, h_ref, acc_ref):
    del te_ref
    i = pl.program_id(0)
    j = pl.program_id(1)
    last = pl.num_programs(1) - 1
    valid = i < nv_ref[0]

    @pl.when(valid)
    def _():
        @pl.when(j == 0)
        def _():
            h_ref[...] = xs_ref[...].astype(BF16)

        h = h_ref[...]
        g = _dot(h, wg_ref[0])
        u = _dot(h, wu_ref[0])
        d = _dot((g * _sigmoid(g) * u).astype(BF16), wd_ref[0])

        @pl.when(j == 0)
        def _():
            acc_ref[...] = d

        @pl.when(j > 0)
        def _():
            acc_ref[...] += d

        @pl.when(j == last)
        def _():
            ys_ref[...] = acc_ref[...]

    @pl.when(jnp.logical_and(jnp.logical_not(valid), j == last))
    def _():
        ys_ref[...] = jnp.zeros_like(ys_ref)


def _moe_ffn(tile_expert, n_valid, xs, wg, wu, wd, tm, tf):
    rows, d = xs.shape
    ff = wg.shape[2]
    grid_spec = pltpu.PrefetchScalarGridSpec(
        num_scalar_prefetch=2,
        grid=(rows // tm, ff // tf),
        in_specs=[pl.BlockSpec((tm, d), lambda i, j, te, nv: (jnp.minimum(i, nv[0] - 1), 0)),
                  pl.BlockSpec((1, d, tf), lambda i, j, te, nv: (te[i], 0, j)),
                  pl.BlockSpec((1, d, tf), lambda i, j, te, nv: (te[i], 0, j)),
                  pl.BlockSpec((1, tf, d), lambda i, j, te, nv: (te[i], j, 0))],
        out_specs=pl.BlockSpec((tm, d), lambda i, j, te, nv: (i, 0)),
        scratch_shapes=[pltpu.VMEM((tm, d), BF16), pltpu.VMEM((tm, d), F32)],
    )
    return pl.pallas_call(
        _moe_ffn_kernel,
        grid_spec=grid_spec,
        out_shape=jax.ShapeDtypeStruct((rows, d), F32),
        compiler_params=_params("arbitrary", "arbitrary"),
        name="moe_ffn",
    )(tile_expert, n_valid, xs, wg, wu, wd)


def _combine_kernel(pos_ref, ys_ref, x_ref, meta_ref, nw_ref, o_ref, y0_ref, y1_ref, sem, *, tb):
    def copies(t):
        return (_row_copy(ys_ref, pos_ref[2 * t], y0_ref, t, sem),
                _row_copy(ys_ref, pos_ref[2 * t + 1], y1_ref, t, sem))

    def issue(g, carry):
        for u in range(ROW_DMA_UNROLL):
            for k, cp in enumerate(copies(g * ROW_DMA_UNROLL + u)):
                cp.start(priority=k)
        return carry

    lax.fori_loop(0, tb // ROW_DMA_UNROLL, issue, 0)

    def drain(g, carry):
        for u in range(ROW_DMA_UNROLL):
            for cp in copies(g * ROW_DMA_UNROLL + u):
                cp.wait()
        return carry

    lax.fori_loop(0, tb // ROW_DMA_UNROLL, drain, 0)

    meta = meta_ref[...]
    y = meta[:, 2:3] * y0_ref[...] + meta[:, 3:4] * y1_ref[...]
    o_ref[...] = x_ref[...] + _rms_rows(y, nw_ref[...])


def _combine(pos_flat, ys, x2, meta, nw, tb=512):
    n, d = x2.shape
    return pl.pallas_call(
        functools.partial(_combine_kernel, tb=tb),
        grid=(n // tb,),
        in_specs=[pl.BlockSpec((2 * tb,), lambda i: (i,), memory_space=pltpu.SMEM),
                  pl.BlockSpec(memory_space=pl.ANY),
                  pl.BlockSpec((tb, d), lambda i: (i, 0)),
                  pl.BlockSpec((tb, LANES), lambda i: (i, 0)),
                  pl.BlockSpec((1, d), lambda i: (0, 0))],
        out_specs=pl.BlockSpec((tb, d), lambda i: (i, 0)),
        out_shape=jax.ShapeDtypeStruct((n, d), F32),
        scratch_shapes=[pltpu.VMEM((tb, d), F32), pltpu.VMEM((tb, d), F32),
                        pltpu.SemaphoreType.DMA(())],
        compiler_params=_params("arbitrary"),
        name="moe_combine",
    )(pos_flat, ys, x2, meta, nw)


def _moe_layer(x2, pre, post, router_w, wg, wu, wd, tm=512, tf=1792):
    n, d = x2.shape
    r_pad = jnp.pad(router_w, ((0, 0), (0, LANES - N_EXPERTS)))
    r_hi = r_pad.astype(BF16)
    r_lo = (r_pad - r_hi.astype(F32)).astype(BF16)
    h, meta, cnt = _router(x2, pre, r_hi, r_lo)

    idx = meta[:, 0:2].astype(jnp.int32)
    rank = meta[:, 4:6].astype(jnp.int32)
    counts = cnt[0, 0:N_EXPERTS].astype(jnp.int32)
    padded = ((counts + tm - 1) // tm) * tm
    ends = jnp.cumsum(padded)
    offsets = ends - padded
    pos_flat = (offsets[idx] + rank).reshape(-1)
    n_tiles = (2 * n) // tm + N_EXPERTS
    n_valid = (ends[-1:] // tm).astype(jnp.int32)
    tile_start = jnp.minimum(jnp.arange(n_tiles, dtype=jnp.int32), n_valid - 1) * tm
    tile_expert = jnp.sum((tile_start[:, None] >= ends[None, :]).astype(jnp.int32), axis=1)

    xs = _dispatch(pos_flat, ends.astype(jnp.int32), h, n_tiles * tm, tm)
    ys = _moe_ffn(tile_expert, n_valid, xs, wg.astype(BF16), wu.astype(BF16), wd.astype(BF16),
                  tm, tf)
    return _combine(pos_flat, ys, x2, meta, post)


def kernel(x, norm_mix_pre, norm_mix_post, norm_ffn_pre, norm_ffn_post, w_in, conv_w, dn_a_log,
           dn_dt_bias, dn_norm_w, sb_norm_w, w_out, ffn_w_gate, ffn_w_up, ffn_w_down, router_w,
           moe_w_gate, moe_w_up, moe_w_down):
    batch, seq, d = x.shape
    depth = w_in.shape[0]
    x2 = x.reshape(batch * seq, d)
    gate_cols = 4 * DN_WIDTH
    n_small = 2 * DN_HEADS
    sb_col0 = gate_cols // LANES

    def pad_lanes(v, lane0):
        return jnp.zeros((1, LANES), F32).at[0, lane0:lane0 + v.shape[0]].set(v)

    for layer in range(depth):
        w = w_in[layer]
        w_big = jnp.concatenate([w[:, :gate_cols], w[:, gate_cols + n_small:]], axis=1).astype(BF16)
        w_small = jnp.pad(w[:, gate_cols:gate_cols + n_small],
                          ((0, 0), (0, LANES - n_small))).astype(BF16)
        proj, small = _in_proj(x2, norm_mix_pre[layer][None, :], w_big, w_small, conv_w[layer],
                               seq)
        dn = _deltanet(proj, small, pad_lanes(dn_a_log[layer], DN_HEADS),
                       pad_lanes(dn_dt_bias[layer], DN_HEADS), dn_norm_w[layer][None, :],
                       batch, seq)
        sb = _sb_attention(proj, jnp.tile(sb_norm_w[layer], 2)[None, :], batch, seq, sb_col0)
        x2 = _out_proj(dn, sb, x2, w_out[layer].astype(BF16), norm_mix_post[layer][None, :])

        pre = norm_ffn_pre[layer][None, :]
        post = norm_ffn_post[layer][None, :]
        i = layer // 2
        if layer % 2 == 0:
            x2 = _dense_ffn(x2, pre, post, ffn_w_gate[i].astype(BF16), ffn_w_up[i].astype(BF16),
                            ffn_w_down[i].astype(BF16))
        else:
            x2 = _moe_layer(x2, pre, post, router_w[i], moe_w_gate[i], moe_w_up[i], moe_w_down[i])
    return x2.reshape(batch, seq, d)
```

```python
import functools

import jax
import jax.numpy as jnp
from jax import lax
from jax.experimental import pallas as pl
from jax.experimental.pallas import tpu as pltpu

F32 = jnp.float32
BF16 = jnp.bfloat16
EPS = 1e-6
LOG2E = 1.4426950408889634
MASKED_SCORE = -1e30

LANES = 128
DN_HEADS = 4
DN_HEAD_DIM = 128
DN_WIDTH = DN_HEADS * DN_HEAD_DIM
SB_HEAD_DIM = 64
SB_WIDTH = 512
SB_PAIRS = SB_WIDTH // LANES
CONV_K = 4
CHUNK = 64
N_EXPERTS = 8
ROW_DMA_UNROLL = 8
FF_SUBCHUNK = 512
VMEM_LIMIT = 56 * 1024 * 1024


def _dot(a, b):
    return jnp.dot(a, b, preferred_element_type=F32)


def _dot_nt(a, b):
    return lax.dot_general(a, b, (((1,), (1,)), ((), ())), preferred_element_type=F32)


def _sigmoid(x):
    return 1.0 / (1.0 + jnp.exp(-x))


def _softplus(x):
    return jnp.maximum(x, 0.0) + jnp.log1p(jnp.exp(-jnp.abs(x)))


def _rms_rows(x, w):
    ms = jnp.mean(x * x, axis=-1, keepdims=True)
    return x * lax.rsqrt(ms + EPS) * w


def _split3(x):
    hi = x.astype(BF16)
    r1 = x - hi.astype(F32)
    mid = r1.astype(BF16)
    lo = (r1 - mid.astype(F32)).astype(BF16)
    return hi, mid, lo


def _params(*sem):
    return pltpu.CompilerParams(dimension_semantics=sem, vmem_limit_bytes=VMEM_LIMIT)


def _in_proj_kernel(x_ref, nw_ref, w_ref, ws_ref, cw_ref, o_ref, os_ref, hist_ref,
                    *, col_chunk, conv_chunks, tiles_per_seq):
    tm = x_ref.shape[0]

    @pl.when(pl.program_id(0) % tiles_per_seq == 0)
    def _():
        hist_ref[0:8, :] = jnp.zeros((8, hist_ref.shape[1]), F32)

    h = _rms_rows(x_ref[...], nw_ref[...]).astype(BF16)
    for c in range(w_ref.shape[1] // col_chunk):
        cols = slice(c * col_chunk, (c + 1) * col_chunk)
        r = _dot(h, w_ref[:, cols])
        if c >= conv_chunks:
            o_ref[:, cols] = r.astype(BF16)
            continue
        hist_ref[8:8 + tm, cols] = r
        acc = r * cw_ref[CONV_K - 1:CONV_K, cols]
        for i in range(CONV_K - 1):
            acc = acc + hist_ref[5 + i:5 + i + tm, cols] * cw_ref[i:i + 1, cols]
        hist_ref[0:8, cols] = hist_ref[tm:tm + 8, cols]
        o_ref[:, cols] = (acc * _sigmoid(acc)).astype(BF16)
    os_ref[...] = _dot(h, ws_ref[...])


def _in_proj(x2, nw, w_big, w_small, conv_w, seq, tm=512):
    n, d = x2.shape
    p = w_big.shape[1]
    conv_cols = conv_w.shape[1]
    return pl.pallas_call(
        functools.partial(_in_proj_kernel, col_chunk=512, conv_chunks=conv_cols // 512,
                          tiles_per_seq=seq // tm),
        grid=(n // tm,),
        in_specs=[pl.BlockSpec((tm, d), lambda i: (i, 0)),
                  pl.BlockSpec((1, d), lambda i: (0, 0)),
                  pl.BlockSpec((d, p), lambda i: (0, 0)),
                  pl.BlockSpec((d, LANES), lambda i: (0, 0)),
                  pl.BlockSpec((CONV_K, conv_cols), lambda i: (0, 0))],
        out_specs=[pl.BlockSpec((tm, p), lambda i: (i, 0)),
                   pl.BlockSpec((tm, LANES), lambda i: (i, 0))],
        out_shape=[jax.ShapeDtypeStruct((n, p), BF16),
                   jax.ShapeDtypeStruct((n, LANES), F32)],
        scratch_shapes=[pltpu.VMEM((tm + 8, conv_cols), F32)],
        compiler_params=_params("arbitrary"),
        name="in_proj",
    )(x2, nw, w_big, w_small, conv_w)


def _deltanet_kernel(q_ref, k_ref, v_ref, z_ref, s_ref, alog_ref, dtb_ref, nw_ref,
                     o_ref, state_ref, *, tc):
    hd = DN_HEAD_DIM
    st = DN_HEADS * CHUNK

    @pl.when(pl.program_id(1) == 0)
    def _init():
        state_ref[...] = jnp.zeros_like(state_ref)

    def l2n(a):
        return a * lax.rsqrt(jnp.sum(a * a, axis=-1, keepdims=True) + EPS)

    def head(ref, h):
        return ref[:, h * hd:(h + 1) * hd].astype(F32)

    qn = [l2n(head(q_ref, h)) * (hd ** -0.5) for h in range(DN_HEADS)]
    kn = [l2n(head(k_ref, h)) for h in range(DN_HEADS)]
    vv = [head(v_ref, h) for h in range(DN_HEADS)]

    sm = s_ref[...]
    beta_t = _sigmoid(sm)
    g_t = -jnp.exp(alog_ref[...]) * _softplus(sm + dtb_ref[...])

    r = lax.broadcasted_iota(jnp.int32, (tc, tc), 0)
    c = lax.broadcasted_iota(jnp.int32, (tc, tc), 1)
    same = (r // CHUNK) == (c // CHUNK)
    cum_blk = jnp.where(same, jnp.where(c <= r, 1.0, 0.0), 0.0).astype(BF16)
    tot_blk = jnp.where(same, 1.0, 0.0).astype(BF16)
    g3 = jnp.concatenate(_split3(g_t), axis=1)
    red = _dot(jnp.concatenate([cum_blk, tot_blk], axis=0), g3)
    red = red[:, 0:LANES] + red[:, LANES:2 * LANES] + red[:, 2 * LANES:3 * LANES]
    gc_t, gl_t = red[0:tc], red[tc:2 * tc]

    rs = lax.broadcasted_iota(jnp.int32, (st, st), 0)
    cs = lax.broadcasted_iota(jnp.int32, (st, st), 1)
    same_s = (rs // CHUNK) == (cs // CHUNK)
    mask_incl = jnp.logical_and(same_s, cs <= rs)
    mask_strict = jnp.logical_and(same_s, cs < rs)
    eye = jnp.where(rs == cs, 1.0, 0.0).astype(F32)
    nw = nw_ref[...]

    n_chunks = tc // CHUNK
    head_of_row = lax.broadcasted_iota(jnp.int32, (st, hd), 0) // CHUNK

    pre = []
    for ci in range(n_chunks):
        rows = slice(ci * CHUNK, (ci + 1) * CHUNK)

        def stack(parts):
            return jnp.concatenate([p[rows] for p in parts], axis=0)

        def stack_col(a, lane0):
            return jnp.concatenate(
                [jnp.broadcast_to(a[rows, lane0 + h:lane0 + h + 1], (CHUNK, hd))
                 for h in range(DN_HEADS)], axis=0)

        qs, ks, vs = stack(qn), stack(kn), stack(vv)
        bs = stack_col(beta_t, 0)
        gc = stack_col(gc_t, DN_HEADS)
        gl = stack_col(gl_t, DN_HEADS)
        kb = ks * bs
        vb = vs * bs
        eg = jnp.exp(gc)
        kbe = kb * eg

        gct = gc.T
        diff = jnp.concatenate([gc, gc], axis=1) - jnp.concatenate([gct, gct], axis=0)
        dm = jnp.exp(jnp.where(mask_incl, diff, -jnp.inf))

        m1 = _dot_nt(jnp.concatenate([kb, qs], axis=0).astype(BF16), ks.astype(BF16))
        lmat = jnp.where(mask_strict, m1[0:st] * dm, 0.0)
        pre.append(dict(
            rows=rows,
            pmat=eye - lmat,
            xb=lmat.astype(BF16),
            rhs=jnp.concatenate([vb, kbe], axis=1).astype(BF16),
            attn=(m1[st:2 * st] * dm).astype(BF16),
            qd_b=(qs * eg).astype(BF16),
            kend_t=(ks * jnp.exp(gl - gc)).T.astype(BF16),
            egl=jnp.exp(gl)))

    for _ in range(5):
        for c in pre:
            c["xb"] = _dot(c["xb"], c["xb"]).astype(BF16)
        for c in pre:
            c["pmat"] = c["pmat"] + _dot(c["pmat"].astype(BF16), c["xb"])
    for c in pre:
        c["uw"] = _dot(c["pmat"].astype(BF16), c["rhs"])

    for c in pre:
        rows, uw, attn, qd_b, egl = c["rows"], c["uw"], c["attn"], c["qd_b"], c["egl"]
        u, wmat = uw[:, 0:hd], uw[:, hd:2 * hd].astype(BF16)

        s_cat = state_ref[...]
        s_b = s_cat.astype(BF16)
        vn = jnp.concatenate(
            [u[h * CHUNK:(h + 1) * CHUNK]
             - _dot(wmat[h * CHUNK:(h + 1) * CHUNK], s_b[:, h * hd:(h + 1) * hd])
             for h in range(DN_HEADS)], axis=0)
        vn_b = vn.astype(BF16)
        o_intra = _dot(attn, vn_b)

        vn_bd = jnp.concatenate(
            [jnp.where(head_of_row == h, vn_b, jnp.zeros_like(vn_b)) for h in range(DN_HEADS)],
            axis=1)
        decay = jnp.concatenate(
            [jnp.concatenate([egl[h * CHUNK:(h + 1) * CHUNK]] * (hd // CHUNK), axis=0)
             for h in range(DN_HEADS)], axis=1)
        state_ref[...] = s_cat * decay + _dot(c["kend_t"], vn_bd)

        for h in range(DN_HEADS):
            hs = slice(h * CHUNK, (h + 1) * CHUNK)
            o_h = _dot(qd_b[hs], s_b[:, h * hd:(h + 1) * hd]) + o_intra[hs]
            zh = z_ref[rows, h * hd:(h + 1) * hd].astype(F32)
            o_ref[rows, h * hd:(h + 1) * hd] = (
                _rms_rows(o_h, nw) * (zh * _sigmoid(zh))).astype(BF16)


def _deltanet(proj, small, alog_pad, dtb_pad, dn_norm_w, batch, seq, tc=512):
    n = proj.shape[0]
    nt = seq // tc
    row = lambda b, t: b * nt + t
    return pl.pallas_call(
        functools.partial(_deltanet_kernel, tc=tc),
        grid=(batch, nt),
        in_specs=[pl.BlockSpec((tc, DN_WIDTH), lambda b, t: (row(b, t), 0)),
                  pl.BlockSpec((tc, DN_WIDTH), lambda b, t: (row(b, t), 1)),
                  pl.BlockSpec((tc, DN_WIDTH), lambda b, t: (row(b, t), 2)),
                  pl.BlockSpec((tc, DN_WIDTH), lambda b, t: (row(b, t), 3)),
                  pl.BlockSpec((tc, LANES), lambda b, t: (row(b, t), 0)),
                  pl.BlockSpec((1, LANES), lambda b, t: (0, 0)),
                  pl.BlockSpec((1, LANES), lambda b, t: (0, 0)),
                  pl.BlockSpec((1, DN_HEAD_DIM), lambda b, t: (0, 0))],
        out_specs=pl.BlockSpec((tc, DN_WIDTH), lambda b, t: (row(b, t), 0)),
        out_shape=jax.ShapeDtypeStruct((n, DN_WIDTH), BF16),
        scratch_shapes=[pltpu.VMEM((DN_HEAD_DIM, DN_WIDTH), F32)],
        compiler_params=_params("parallel", "arbitrary"),
        name="deltanet",
    )(proj, proj, proj, proj, small, alog_pad, dtb_pad, dn_norm_w)


def _sb_kernel(q_ref, k_ref, v_ref, nw_ref, o_ref, qst_ref, total_ref, acc_ref, *, tq, nq):
    tk = tq
    i = pl.program_id(2)
    lane = lax.broadcasted_iota(jnp.int32, (1, LANES), 1)
    first = lane < SB_HEAD_DIM
    last_lane = lane == LANES - 1
    blocks = (i, nq - 1 - i)

    for s in range(2):
        q2 = q_ref[pl.ds(pl.multiple_of(blocks[s] * tq, tq), tq), :] * (SB_HEAD_DIM ** -0.5)
        zero = jnp.zeros_like(q2)
        qst_ref[s] = jnp.concatenate([jnp.where(first, q2, zero), jnp.where(first, zero, q2)],
                                     axis=0)
    total_ref[...] = jnp.zeros_like(total_ref)
    acc_ref[...] = jnp.zeros_like(acc_ref)

    rq = lax.broadcasted_iota(jnp.int32, (2 * tq, tk), 0)
    ck = lax.broadcasted_iota(jnp.int32, (2 * tq, tk), 1)
    diag_bias = jnp.where(ck < jnp.where(rq >= tq, rq - tq, rq), 0.0, MASKED_SCORE)

    rj = lax.broadcasted_iota(jnp.int32, (tk, tk), 0)
    cs = lax.broadcasted_iota(jnp.int32, (tk, tk), 1)
    suffix = jnp.where(rj >= cs, 1.0, 0.0).astype(BF16)

    def visit(v):
        if v < 2:
            slot, tile = v, blocks[v]
        else:
            second = v - 2 >= i
            slot = second.astype(jnp.int32)
            tile = jnp.where(second, nq - v, i + 1 - v)
        return slot, pl.multiple_of(tile * tk, tk)

    def scores(v):
        slot, start = visit(v)
        z = _dot_nt(qst_ref[slot], k_ref[pl.ds(start, tk), :])
        return z + diag_bias if v < 2 else z

    def weights(v, z):
        slot, _ = visit(v)
        drop = jnp.maximum(z, 0.0) + jnp.log(1.0 + jnp.exp2(jnp.abs(z) * -LOG2E))
        parked = jnp.where(last_lane, pltpu.roll(total_ref[slot], LANES - 1, 1), 0.0)
        drop = jnp.concatenate([drop[:, 0:tk - LANES], drop[:, tk - LANES:tk] + parked], axis=1)
        inc = _dot(drop.astype(BF16), suffix)
        total_ref[slot] = inc[:, 0:LANES]
        return jnp.exp2((z - inc) * LOG2E).astype(BF16)

    def output(v, w):
        slot, start = visit(v)
        acc_ref[slot] += _dot(w, v_ref[pl.ds(start, tk), :])

    n_visits = nq + 1
    z, w = {}, {}
    for n in range(n_visits + 2):
        if n < n_visits:
            z[n] = scores(n)
        if 1 <= n <= n_visits:
            w[n - 1] = weights(n - 1, z.pop(n - 1))
        if n >= 2:
            output(n - 2, w.pop(n - 2))

    for s in range(2):
        acc = acc_ref[s]
        o2 = jnp.where(first, acc[0:tq], acc[tq:2 * tq])
        sq = o2 * o2
        s_first = jnp.sum(jnp.where(first, sq, 0.0), axis=-1, keepdims=True)
        s_all = jnp.sum(sq, axis=-1, keepdims=True)
        ms = jnp.where(first, s_first, s_all - s_first) * (1.0 / SB_HEAD_DIM)
        o_ref[pl.ds(pl.multiple_of(blocks[s] * tq, tq), tq), :] = (
            o2 * lax.rsqrt(ms + EPS) * nw_ref[...]).astype(BF16)


def _sb_attention(proj, sb_norm_w2, batch, seq, col0, tq=256):
    n = proj.shape[0]
    nq = seq // tq
    assert nq % 2 == 0
    seq_block = lambda c: pl.BlockSpec((seq, LANES), lambda b, p, i: (b, c + p))
    return pl.pallas_call(
        functools.partial(_sb_kernel, tq=tq, nq=nq),
        grid=(batch, SB_PAIRS, nq // 2),
        in_specs=[seq_block(col0), seq_block(col0 + SB_PAIRS), seq_block(col0 + 2 * SB_PAIRS),
                  pl.BlockSpec((1, LANES), lambda b, p, i: (0, 0))],
        out_specs=pl.BlockSpec((seq, LANES), lambda b, p, i: (b, p)),
        out_shape=jax.ShapeDtypeStruct((n, SB_WIDTH), BF16),
        scratch_shapes=[pltpu.VMEM((2, 2 * tq, LANES), BF16),
                        pltpu.VMEM((2, 2 * tq, LANES), F32),
                        pltpu.VMEM((2, 2 * tq, LANES), F32)],
        compiler_params=_params("parallel", "parallel", "arbitrary"),
        name="sb_attention",
    )(proj, proj, proj, sb_norm_w2)


def _out_proj_kernel(dn_ref, sb_ref, x_ref, w_ref, nw_ref, o_ref):
    half = dn_ref.shape[1]
    mix = _dot(dn_ref[...], w_ref[0:half, :]) + _dot(sb_ref[...], w_ref[half:2 * half, :])
    o_ref[...] = x_ref[...] + _rms_rows(mix, nw_ref[...])


def _out_proj(dn, sb, x2, w_out, nw, tm=512):
    n, d = x2.shape
    half = dn.shape[1]
    return pl.pallas_call(
        _out_proj_kernel,
        grid=(n // tm,),
        in_specs=[pl.BlockSpec((tm, half), lambda i: (i, 0)),
                  pl.BlockSpec((tm, half), lambda i: (i, 0)),
                  pl.BlockSpec((tm, d), lambda i: (i, 0)),
                  pl.BlockSpec((2 * half, d), lambda i: (0, 0)),
                  pl.BlockSpec((1, d), lambda i: (0, 0))],
        out_specs=pl.BlockSpec((tm, d), lambda i: (i, 0)),
        out_shape=jax.ShapeDtypeStruct((n, d), F32),
        compiler_params=_params("parallel"),
        name="out_proj",
    )(dn, sb, x2, w_out, nw)


def _swiglu_partial(h, wg, wu, wd, tf):
    out = None
    for c0 in range(0, tf, FF_SUBCHUNK):
        cols = slice(c0, min(c0 + FF_SUBCHUNK, tf))
        g = _dot(h, wg(cols))
        u = _dot(h, wu(cols))
        part = _dot((g * _sigmoid(g) * u).astype(BF16), wd(cols))
        out = part if out is None else out + part
    return out


def _dense_ffn_kernel(x_ref, pre_ref, post_ref, wg_ref, wu_ref, wd_ref, o_ref, h_ref, acc_ref):
    j = pl.program_id(1)

    @pl.when(j == 0)
    def _():
        h_ref[...] = _rms_rows(x_ref[...], pre_ref[...]).astype(BF16)

    d = _swiglu_partial(h_ref[...], lambda c: wg_ref[:, c], lambda c: wu_ref[:, c],
                        lambda c: wd_ref[c, :], wg_ref.shape[1])

    @pl.when(j == 0)
    def _():
        acc_ref[...] = d

    @pl.when(j > 0)
    def _():
        acc_ref[...] += d

    @pl.when(j == pl.num_programs(1) - 1)
    def _():
        o_ref[...] = x_ref[...] + _rms_rows(acc_ref[...], post_ref[...])


def _dense_ffn(x2, pre, post, wg, wu, wd, tm=512, tf=2816):
    n, d = x2.shape
    ff = wg.shape[1]
    return pl.pallas_call(
        _dense_ffn_kernel,
        grid=(n // tm, ff // tf),
        in_specs=[pl.BlockSpec((tm, d), lambda i, j: (i, 0)),
                  pl.BlockSpec((1, d), lambda i, j: (0, 0)),
                  pl.BlockSpec((1, d), lambda i, j: (0, 0)),
                  pl.BlockSpec((d, tf), lambda i, j: (0, j)),
                  pl.BlockSpec((d, tf), lambda i, j: (0, j)),
                  pl.BlockSpec((tf, d), lambda i, j: (j, 0))],
        out_specs=pl.BlockSpec((tm, d), lambda i, j: (i, 0)),
        out_shape=jax.ShapeDtypeStruct((n, d), F32),
        scratch_shapes=[pltpu.VMEM((tm, d), BF16), pltpu.VMEM((tm, d), F32)],
        compiler_params=_params("parallel", "arbitrary"),
        name="dense_ffn",
    )(x2, pre, post, wg, wu, wd)


def _router_kernel(x_ref, nw_ref, rhi_ref, rlo_ref, h_ref, meta_ref, cnt_ref, run_ref):
    tm = x_ref.shape[0]

    @pl.when(pl.program_id(0) == 0)
    def _():
        run_ref[...] = jnp.zeros_like(run_ref)

    hf = _rms_rows(x_ref[...], nw_ref[...])
    h_ref[...] = hf
    hb = hf.astype(BF16)
    hlo = (hf - hb.astype(F32)).astype(BF16)
    logits = _dot(hb, rhi_ref[...]) + _dot(hb, rlo_ref[...]) + _dot(hlo, rhi_ref[...])

    lane = lax.broadcasted_iota(jnp.int32, (tm, LANES), 1).astype(F32)
    lg = jnp.where(lane < N_EXPERTS, logits, -jnp.inf)
    m1 = jnp.max(lg, axis=-1, keepdims=True)
    i1 = jnp.min(jnp.where(lg == m1, lane, float(LANES)), axis=-1, keepdims=True)
    lg2 = jnp.where(lane == i1, -jnp.inf, lg)
    m2 = jnp.max(lg2, axis=-1, keepdims=True)
    i2 = jnp.min(jnp.where(lg2 == m2, lane, float(LANES)), axis=-1, keepdims=True)
    e = jnp.exp(m2 - m1)
    g1 = 1.0 / (1.0 + e)
    g2 = e * g1

    onehot = jnp.where(lane == i1, 1.0, jnp.where(lane == i2, 1.0, 0.0))
    r = lax.broadcasted_iota(jnp.int32, (tm, tm), 0)
    c = lax.broadcasted_iota(jnp.int32, (tm, tm), 1)
    before = jnp.where(c < r, 1.0, 0.0).astype(BF16)
    rank = _dot(before, onehot.astype(BF16)) + run_ref[...]
    r1 = jnp.sum(jnp.where(lane == i1, rank, 0.0), axis=-1, keepdims=True)
    r2 = jnp.sum(jnp.where(lane == i2, rank, 0.0), axis=-1, keepdims=True)
    run_ref[...] += jnp.sum(onehot, axis=0, keepdims=True)
    cnt_ref[...] = run_ref[...]

    meta = jnp.zeros((tm, LANES), F32)
    for k, val in enumerate((i1, i2, g1, g2, r1, r2)):
        meta = jnp.where(lane == float(k), val, meta)
    meta_ref[...] = meta


def _router(x2, nw, r_hi, r_lo, tm=512):
    n, d = x2.shape
    return pl.pallas_call(
        _router_kernel,
        grid=(n // tm,),
        in_specs=[pl.BlockSpec((tm, d), lambda i: (i, 0)),
                  pl.BlockSpec((1, d), lambda i: (0, 0)),
                  pl.BlockSpec((d, LANES), lambda i: (0, 0)),
                  pl.BlockSpec((d, LANES), lambda i: (0, 0))],
        out_specs=[pl.BlockSpec((tm, d), lambda i: (i, 0)),
                   pl.BlockSpec((tm, LANES), lambda i: (i, 0)),
                   pl.BlockSpec((1, LANES), lambda i: (0, 0))],
        out_shape=[jax.ShapeDtypeStruct((n, d), F32),
                   jax.ShapeDtypeStruct((n, LANES), F32),
                   jax.ShapeDtypeStruct((1, LANES), F32)],
        scratch_shapes=[pltpu.VMEM((1, LANES), F32)],
        compiler_params=_params("arbitrary"),
        name="router",
    )(x2, nw, r_hi, r_lo)


def _row_copy(src_ref, src_row, dst_ref, dst_row, sem):
    return pltpu.make_async_copy(src_ref.at[pl.ds(src_row, 1)], dst_ref.at[pl.ds(dst_row, 1)], sem)


def _dispatch_kernel(pos_ref, ends_ref, h_ref, xs_ref, zero_ref, sem, zero_sem, *, tb, tm):
    @pl.when(pl.program_id(0) == 0)
    def _():
        zero_ref[...] = jnp.zeros_like(zero_ref)

        def fill(e):
            if e < N_EXPERTS:
                start = ends_ref[e] - tm
                wanted = start >= (ends_ref[e - 1] if e else 0)
            else:
                start = ends_ref[N_EXPERTS - 1] + (e - N_EXPERTS) * tm
                wanted = start < xs_ref.shape[0]
            start = pl.multiple_of(jnp.clip(start, 0, xs_ref.shape[0] - tm), tm)
            return wanted, pltpu.make_async_copy(zero_ref, xs_ref.at[pl.ds(start, tm)], zero_sem)

        for e in range(2 * N_EXPERTS):
            wanted, cp = fill(e)
            pl.when(wanted)(cp.start)
        for e in range(2 * N_EXPERTS):
            wanted, cp = fill(e)
            pl.when(wanted)(cp.wait)

    def copies(t):
        return (_row_copy(h_ref, t, xs_ref, pos_ref[2 * t], sem),
                _row_copy(h_ref, t, xs_ref, pos_ref[2 * t + 1], sem))

    def issue(g, carry):
        for u in range(ROW_DMA_UNROLL):
            for k, cp in enumerate(copies(g * ROW_DMA_UNROLL + u)):
                cp.start(priority=k)
        return carry

    lax.fori_loop(0, tb // ROW_DMA_UNROLL, issue, 0)

    def drain(g, carry):
        for u in range(ROW_DMA_UNROLL):
            for cp in copies(g * ROW_DMA_UNROLL + u):
                cp.wait()
        return carry

    lax.fori_loop(0, tb // ROW_DMA_UNROLL, drain, 0)


def _dispatch(pos_flat, ends, h, rows, tm, tb=512):
    n, d = h.shape
    return pl.pallas_call(
        functools.partial(_dispatch_kernel, tb=tb, tm=tm),
        grid=(n // tb,),
        in_specs=[pl.BlockSpec((2 * tb,), lambda i: (i,), memory_space=pltpu.SMEM),
                  pl.BlockSpec(memory_space=pltpu.SMEM),
                  pl.BlockSpec((tb, d), lambda i: (i, 0))],
        out_specs=pl.BlockSpec(memory_space=pl.ANY),
        out_shape=jax.ShapeDtypeStruct((rows, d), h.dtype),
        scratch_shapes=[pltpu.VMEM((tm, d), h.dtype), pltpu.SemaphoreType.DMA(()),
                        pltpu.SemaphoreType.DMA(())],
        compiler_params=_params("arbitrary"),
        name="moe_dispatch",
    )(pos_flat, ends, h)


def _moe_ffn_kernel(te_ref, nv_ref, xs_ref, wg_ref, wu_ref, wd_ref, ys_ref, h_ref, acc_ref):
    del te_ref
    i = pl.program_id(0)
    j = pl.program_id(1)
    last = pl.num_programs(1) - 1
    valid = i < nv_ref[0]

    @pl.when(valid)
    def _():
        @pl.when(j == 0)
        def _():
            h_ref[...] = xs_ref[...].astype(BF16)

        d = _swiglu_partial(h_ref[...], lambda c: wg_ref[0, :, c], lambda c: wu_ref[0, :, c],
                            lambda c: wd_ref[0, c, :], wg_ref.shape[2])

        @pl.when(j == 0)
        def _():
            acc_ref[...] = d

        @pl.when(j > 0)
        def _():
            acc_ref[...] += d

        @pl.when(j == last)
        def _():
            ys_ref[...] = acc_ref[...]

    @pl.when(jnp.logical_and(jnp.logical_not(valid), j == last))
    def _():
        ys_ref[...] = jnp.zeros_like(ys_ref)


def _moe_ffn(tile_expert, n_valid, xs, wg, wu, wd, tm, tf):
    rows, d = xs.shape
    ff = wg.shape[2]
    grid_spec = pltpu.PrefetchScalarGridSpec(
        num_scalar_prefetch=2,
        grid=(rows // tm, ff // tf),
        in_specs=[pl.BlockSpec((tm, d), lambda i, j, te, nv: (jnp.minimum(i, nv[0] - 1), 0)),
                  pl.BlockSpec((1, d, tf), lambda i, j, te, nv: (te[i], 0, j)),
                  pl.BlockSpec((1, d, tf), lambda i, j, te, nv: (te[i], 0, j)),
                  pl.BlockSpec((1, tf, d), lambda i, j, te, nv: (te[i], j, 0))],
        out_specs=pl.BlockSpec((tm, d), lambda i, j, te, nv: (i, 0)),
        scratch_shapes=[pltpu.VMEM((tm, d), BF16), pltpu.VMEM((tm, d), F32)],
    )
    return pl.pallas_call(
        _moe_ffn_kernel,
        grid_spec=grid_spec,
        out_shape=jax.ShapeDtypeStruct((rows, d), F32),
        compiler_params=_params("arbitrary", "arbitrary"),
        name="moe_ffn",
    )(tile_expert, n_valid, xs, wg, wu, wd)


def _combine_kernel(pos_ref, ys_ref, x_ref, meta_ref, nw_ref, o_ref, y0_ref, y1_ref, sem, *, tb):
    def copies(t):
        return (_row_copy(ys_ref, pos_ref[2 * t], y0_ref, t, sem),
                _row_copy(ys_ref, pos_ref[2 * t + 1], y1_ref, t, sem))

    def issue(g, carry):
        for u in range(ROW_DMA_UNROLL):
            for k, cp in enumerate(copies(g * ROW_DMA_UNROLL + u)):
                cp.start(priority=k)
        return carry

    lax.fori_loop(0, tb // ROW_DMA_UNROLL, issue, 0)

    def drain(g, carry):
        for u in range(ROW_DMA_UNROLL):
            for cp in copies(g * ROW_DMA_UNROLL + u):
                cp.wait()
        return carry

    lax.fori_loop(0, tb // ROW_DMA_UNROLL, drain, 0)

    meta = meta_ref[...]
    y = meta[:, 2:3] * y0_ref[...] + meta[:, 3:4] * y1_ref[...]
    o_ref[...] = x_ref[...] + _rms_rows(y, nw_ref[...])


def _combine(pos_flat, ys, x2, meta, nw, tb=512):
    n, d = x2.shape
    return pl.pallas_call(
        functools.partial(_combine_kernel, tb=tb),
        grid=(n // tb,),
        in_specs=[pl.BlockSpec((2 * tb,), lambda i: (i,), memory_space=pltpu.SMEM),
                  pl.BlockSpec(memory_space=pl.ANY),
                  pl.BlockSpec((tb, d), lambda i: (i, 0)),
                  pl.BlockSpec((tb, LANES), lambda i: (i, 0)),
                  pl.BlockSpec((1, d), lambda i: (0, 0))],
        out_specs=pl.BlockSpec((tb, d), lambda i: (i, 0)),
        out_shape=jax.ShapeDtypeStruct((n, d), F32),
        scratch_shapes=[pltpu.VMEM((tb, d), F32), pltpu.VMEM((tb, d), F32),
                        pltpu.SemaphoreType.DMA(())],
        compiler_params=_params("arbitrary"),
        name="moe_combine",
    )(pos_flat, ys, x2, meta, nw)


def _moe_layer(x2, pre, post, router_w, wg, wu, wd, tm=512, tf=3584):
    n, d = x2.shape
    r_pad = jnp.pad(router_w, ((0, 0), (0, LANES - N_EXPERTS)))
    r_hi = r_pad.astype(BF16)
    r_lo = (r_pad - r_hi.astype(F32)).astype(BF16)
    h, meta, cnt = _router(x2, pre, r_hi, r_lo)

    idx = meta[:, 0:2].astype(jnp.int32)
    rank = meta[:, 4:6].astype(jnp.int32)
    counts = cnt[0, 0:N_EXPERTS].astype(jnp.int32)
    padded = ((counts + tm - 1) // tm) * tm
    ends = jnp.cumsum(padded)
    offsets = ends - padded
    pos_flat = (offsets[idx] + rank).reshape(-1)
    n_tiles = (2 * n) // tm + N_EXPERTS
    n_valid = (ends[-1:] // tm).astype(jnp.int32)
    tile_start = jnp.minimum(jnp.arange(n_tiles, dtype=jnp.int32), n_valid - 1) * tm
    tile_expert = jnp.sum((tile_start[:, None] >= ends[None, :]).astype(jnp.int32), axis=1)

    xs = _dispatch(pos_flat, ends.astype(jnp.int32), h, n_tiles * tm, tm)
    ys = _moe_ffn(tile_expert, n_valid, xs, wg.astype(BF16), wu.astype(BF16), wd.astype(BF16),
                  tm, tf)
    return _combine(pos_flat, ys, x2, meta, post)


def kernel(x, norm_mix_pre, norm_mix_post, norm_ffn_pre, norm_ffn_post, w_in, conv_w, dn_a_log,
           dn_dt_bias, dn_norm_w, sb_norm_w, w_out, ffn_w_gate, ffn_w_up, ffn_w_down, router_w,
           moe_w_gate, moe_w_up, moe_w_down):
    batch, seq, d = x.shape
    depth = w_in.shape[0]
    x2 = x.reshape(batch * seq, d)
    gate_cols = 4 * DN_WIDTH
    n_small = 2 * DN_HEADS
    sb_col0 = gate_cols // LANES

    def pad_lanes(v, lane0):
        return jnp.zeros((1, LANES), F32).at[0, lane0:lane0 + v.shape[0]].set(v)

    for layer in range(depth):
        w = w_in[layer]
        w_big = jnp.concatenate([w[:, :gate_cols], w[:, gate_cols + n_small:]], axis=1).astype(BF16)
        w_small = jnp.pad(w[:, gate_cols:gate_cols + n_small],
                          ((0, 0), (0, LANES - n_small))).astype(BF16)
        proj, small = _in_proj(x2, norm_mix_pre[layer][None, :], w_big, w_small, conv_w[layer],
                               seq)
        dn = _deltanet(proj, small, pad_lanes(dn_a_log[layer], DN_HEADS),
                       pad_lanes(dn_dt_bias[layer], DN_HEADS), dn_norm_w[layer][None, :],
                       batch, seq)
        sb = _sb_attention(proj, jnp.tile(sb_norm_w[layer], 2)[None, :], batch, seq, sb_col0)
        x2 = _out_proj(dn, sb, x2, w_out[layer].astype(BF16), norm_mix_post[layer][None, :])

        pre = norm_ffn_pre[layer][None, :]
        post = norm_ffn_post[layer][None, :]
        i = layer // 2
        if layer % 2 == 0:
            x2 = _dense_ffn(x2, pre, post, ffn_w_gate[i].astype(BF16), ffn_w_up[i].astype(BF16),
                            ffn_w_down[i].astype(BF16))
        else:
            x2 = _moe_layer(x2, pre, post, router_w[i], moe_w_gate[i], moe_w_up[i], moe_w_down[i])
    return x2.reshape(batch, seq, d)
```

```python
import functools

import jax
import jax.numpy as jnp
from jax import lax
from jax.experimental import pallas as pl
from jax.experimental.pallas import tpu as pltpu

F32 = jnp.float32
BF16 = jnp.bfloat16
EPS = 1e-6
LOG2E = 1.4426950408889634
MASKED_SCORE = -1e30

LANES = 128
DN_HEADS = 4
DN_HEAD_DIM = 128
DN_WIDTH = DN_HEADS * DN_HEAD_DIM
SB_HEAD_DIM = 64
SB_WIDTH = 512
SB_PAIRS = SB_WIDTH // LANES
CONV_K = 4
CHUNK = 64
N_EXPERTS = 8
ROW_DMA_UNROLL = 8
FF_SUBCHUNK = 512
VMEM_LIMIT = 56 * 1024 * 1024


def _dot(a, b):
    return jnp.dot(a, b, preferred_element_type=F32)


def _dot_nt(a, b):
    return lax.dot_general(a, b, (((1,), (1,)), ((), ())), preferred_element_type=F32)


def _sigmoid(x):
    return 1.0 / (1.0 + jnp.exp(-x))


def _softplus(x):
    return jnp.maximum(x, 0.0) + jnp.log1p(jnp.exp(-jnp.abs(x)))


def _rms_rows(x, w):
    ms = jnp.mean(x * x, axis=-1, keepdims=True)
    return x * lax.rsqrt(ms + EPS) * w


def _split3(x):
    hi = x.astype(BF16)
    r1 = x - hi.astype(F32)
    mid = r1.astype(BF16)
    lo = (r1 - mid.astype(F32)).astype(BF16)
    return hi, mid, lo


def _params(*sem):
    return pltpu.CompilerParams(dimension_semantics=sem, vmem_limit_bytes=VMEM_LIMIT)


def _in_proj_kernel(x_ref, nw_ref, w_ref, ws_ref, cw_ref, o_ref, os_ref, hist_ref,
                    *, col_chunk, conv_chunks, tiles_per_seq):
    tm = x_ref.shape[0]

    @pl.when(pl.program_id(0) % tiles_per_seq == 0)
    def _():
        hist_ref[0:8, :] = jnp.zeros((8, hist_ref.shape[1]), F32)

    h = _rms_rows(x_ref[...], nw_ref[...]).astype(BF16)
    for c in range(w_ref.shape[1] // col_chunk):
        cols = slice(c * col_chunk, (c + 1) * col_chunk)
        r = _dot(h, w_ref[:, cols])
        if c >= conv_chunks:
            o_ref[:, cols] = r.astype(BF16)
            continue
        hist_ref[8:8 + tm, cols] = r
        acc = r * cw_ref[CONV_K - 1:CONV_K, cols]
        for i in range(CONV_K - 1):
            acc = acc + hist_ref[5 + i:5 + i + tm, cols] * cw_ref[i:i + 1, cols]
        hist_ref[0:8, cols] = hist_ref[tm:tm + 8, cols]
        o_ref[:, cols] = (acc * _sigmoid(acc)).astype(BF16)
    os_ref[...] = _dot(h, ws_ref[...])


def _in_proj(x2, nw, w_big, w_small, conv_w, seq, tm=512):
    n, d = x2.shape
    p = w_big.shape[1]
    conv_cols = conv_w.shape[1]
    return pl.pallas_call(
        functools.partial(_in_proj_kernel, col_chunk=512, conv_chunks=conv_cols // 512,
                          tiles_per_seq=seq // tm),
        grid=(n // tm,),
        in_specs=[pl.BlockSpec((tm, d), lambda i: (i, 0)),
                  pl.BlockSpec((1, d), lambda i: (0, 0)),
                  pl.BlockSpec((d, p), lambda i: (0, 0)),
                  pl.BlockSpec((d, LANES), lambda i: (0, 0)),
                  pl.BlockSpec((CONV_K, conv_cols), lambda i: (0, 0))],
        out_specs=[pl.BlockSpec((tm, p), lambda i: (i, 0)),
                   pl.BlockSpec((tm, LANES), lambda i: (i, 0))],
        out_shape=[jax.ShapeDtypeStruct((n, p), BF16),
                   jax.ShapeDtypeStruct((n, LANES), F32)],
        scratch_shapes=[pltpu.VMEM((tm + 8, conv_cols), F32)],
        compiler_params=_params("arbitrary"),
        name="in_proj",
    )(x2, nw, w_big, w_small, conv_w)


def _deltanet_kernel(q_ref, k_ref, v_ref, z_ref, s_ref, alog_ref, dtb_ref, nw_ref,
                     o_ref, state_ref, *, tc):
    hd = DN_HEAD_DIM
    st = DN_HEADS * CHUNK

    @pl.when(pl.program_id(1) == 0)
    def _init():
        state_ref[...] = jnp.zeros_like(state_ref)

    def l2n(a):
        return a * lax.rsqrt(jnp.sum(a * a, axis=-1, keepdims=True) + EPS)

    def head(ref, h):
        return ref[:, h * hd:(h + 1) * hd].astype(F32)

    qn = [l2n(head(q_ref, h)) * (hd ** -0.5) for h in range(DN_HEADS)]
    kn = [l2n(head(k_ref, h)) for h in range(DN_HEADS)]
    vv = [head(v_ref, h) for h in range(DN_HEADS)]

    sm = s_ref[...]
    beta_t = _sigmoid(sm)
    g_t = -jnp.exp(alog_ref[...]) * _softplus(sm + dtb_ref[...])

    r = lax.broadcasted_iota(jnp.int32, (tc, tc), 0)
    c = lax.broadcasted_iota(jnp.int32, (tc, tc), 1)
    same = (r // CHUNK) == (c // CHUNK)
    cum_blk = jnp.where(same, jnp.where(c <= r, 1.0, 0.0), 0.0).astype(BF16)
    tot_blk = jnp.where(same, 1.0, 0.0).astype(BF16)
    g3 = jnp.concatenate(_split3(g_t), axis=1)
    red = _dot(jnp.concatenate([cum_blk, tot_blk], axis=0), g3)
    red = red[:, 0:LANES] + red[:, LANES:2 * LANES] + red[:, 2 * LANES:3 * LANES]
    gc_t, gl_t = red[0:tc], red[tc:2 * tc]

    rs = lax.broadcasted_iota(jnp.int32, (st, st), 0)
    cs = lax.broadcasted_iota(jnp.int32, (st, st), 1)
    same_s = (rs // CHUNK) == (cs // CHUNK)
    mask_incl = jnp.logical_and(same_s, cs <= rs)
    mask_strict = jnp.logical_and(same_s, cs < rs)
    eye = jnp.where(rs == cs, 1.0, 0.0).astype(F32)
    nw = nw_ref[...]

    n_chunks = tc // CHUNK
    head_of_row = lax.broadcasted_iota(jnp.int32, (st, hd), 0) // CHUNK

    pre = []
    for ci in range(n_chunks):
        rows = slice(ci * CHUNK, (ci + 1) * CHUNK)

        def stack(parts):
            return jnp.concatenate([p[rows] for p in parts], axis=0)

        def stack_col(a, lane0):
            return jnp.concatenate(
                [jnp.broadcast_to(a[rows, lane0 + h:lane0 + h + 1], (CHUNK, hd))
                 for h in range(DN_HEADS)], axis=0)

        qs, ks, vs = stack(qn), stack(kn), stack(vv)
        bs = stack_col(beta_t, 0)
        gc = stack_col(gc_t, DN_HEADS)
        gl = stack_col(gl_t, DN_HEADS)
        kb = ks * bs
        vb = vs * bs
        eg = jnp.exp(gc)
        kbe = kb * eg

        gct = gc.T
        diff = jnp.concatenate([gc, gc], axis=1) - jnp.concatenate([gct, gct], axis=0)
        dm = jnp.exp(jnp.where(mask_incl, diff, -jnp.inf))

        m1 = _dot_nt(jnp.concatenate([kb, qs], axis=0).astype(BF16), ks.astype(BF16))
        lmat = jnp.where(mask_strict, m1[0:st] * dm, 0.0)
        pre.append(dict(
            rows=rows,
            pmat=eye - lmat,
            xb=lmat.astype(BF16),
            rhs=jnp.concatenate([vb, kbe], axis=1).astype(BF16),
            attn=(m1[st:2 * st] * dm).astype(BF16),
            qd_b=(qs * eg).astype(BF16),
            kend_t=(ks * jnp.exp(gl - gc)).T.astype(BF16),
            egl=jnp.exp(gl)))

    for _ in range(5):
        for c in pre:
            c["xb"] = _dot(c["xb"], c["xb"]).astype(BF16)
        for c in pre:
            c["pmat"] = c["pmat"] + _dot(c["pmat"].astype(BF16), c["xb"])
    for c in pre:
        c["uw"] = _dot(c["pmat"].astype(BF16), c["rhs"])

    for c in pre:
        rows, uw, attn, qd_b, egl = c["rows"], c["uw"], c["attn"], c["qd_b"], c["egl"]
        u, wmat = uw[:, 0:hd], uw[:, hd:2 * hd].astype(BF16)

        s_cat = state_ref[...]
        s_b = s_cat.astype(BF16)
        vn = jnp.concatenate(
            [u[h * CHUNK:(h + 1) * CHUNK]
             - _dot(wmat[h * CHUNK:(h + 1) * CHUNK], s_b[:, h * hd:(h + 1) * hd])
             for h in range(DN_HEADS)], axis=0)
        vn_b = vn.astype(BF16)
        o_intra = _dot(attn, vn_b)

        vn_bd = jnp.concatenate(
            [jnp.where(head_of_row == h, vn_b, jnp.zeros_like(vn_b)) for h in range(DN_HEADS)],
            axis=1)
        decay = jnp.concatenate(
            [jnp.concatenate([egl[h * CHUNK:(h + 1) * CHUNK]] * (hd // CHUNK), axis=0)
             for h in range(DN_HEADS)], axis=1)
        state_ref[...] = s_cat * decay + _dot(c["kend_t"], vn_bd)

        for h in range(DN_HEADS):
            hs = slice(h * CHUNK, (h + 1) * CHUNK)
            o_h = _dot(qd_b[hs], s_b[:, h * hd:(h + 1) * hd]) + o_intra[hs]
            zh = z_ref[rows, h * hd:(h + 1) * hd].astype(F32)
            o_ref[rows, h * hd:(h + 1) * hd] = (
                _rms_rows(o_h, nw) * (zh * _sigmoid(zh))).astype(BF16)


def _deltanet(proj, small, alog_pad, dtb_pad, dn_norm_w, batch, seq, tc=512):
    n = proj.shape[0]
    nt = seq // tc
    row = lambda b, t: b * nt + t
    return pl.pallas_call(
        functools.partial(_deltanet_kernel, tc=tc),
        grid=(batch, nt),
        in_specs=[pl.BlockSpec((tc, DN_WIDTH), lambda b, t: (row(b, t), 0)),
                  pl.BlockSpec((tc, DN_WIDTH), lambda b, t: (row(b, t), 1)),
                  pl.BlockSpec((tc, DN_WIDTH), lambda b, t: (row(b, t), 2)),
                  pl.BlockSpec((tc, DN_WIDTH), lambda b, t: (row(b, t), 3)),
                  pl.BlockSpec((tc, LANES), lambda b, t: (row(b, t), 0)),
                  pl.BlockSpec((1, LANES), lambda b, t: (0, 0)),
                  pl.BlockSpec((1, LANES), lambda b, t: (0, 0)),
                  pl.BlockSpec((1, DN_HEAD_DIM), lambda b, t: (0, 0))],
        out_specs=pl.BlockSpec((tc, DN_WIDTH), lambda b, t: (row(b, t), 0)),
        out_shape=jax.ShapeDtypeStruct((n, DN_WIDTH), BF16),
        scratch_shapes=[pltpu.VMEM((DN_HEAD_DIM, DN_WIDTH), F32)],
        compiler_params=_params("parallel", "arbitrary"),
        name="deltanet",
    )(proj, proj, proj, proj, small, alog_pad, dtb_pad, dn_norm_w)


def _sb_kernel(q_ref, k_ref, v_ref, nw_ref, o_ref, qst_ref, total_ref, acc_ref, *, tq, nq):
    tk = tq
    i = pl.program_id(2)
    lane = lax.broadcasted_iota(jnp.int32, (1, LANES), 1)
    first = lane < SB_HEAD_DIM
    last_lane = lane == LANES - 1
    blocks = (i, nq - 1 - i)

    for s in range(2):
        q2 = q_ref[pl.ds(pl.multiple_of(blocks[s] * tq, tq), tq), :] * (SB_HEAD_DIM ** -0.5)
        zero = jnp.zeros_like(q2)
        qst_ref[s] = jnp.concatenate([jnp.where(first, q2, zero), jnp.where(first, zero, q2)],
                                     axis=0)
    total_ref[...] = jnp.zeros_like(total_ref)
    acc_ref[...] = jnp.zeros_like(acc_ref)

    rq = lax.broadcasted_iota(jnp.int32, (2 * tq, tk), 0)
    ck = lax.broadcasted_iota(jnp.int32, (2 * tq, tk), 1)
    diag_bias = jnp.where(ck < jnp.where(rq >= tq, rq - tq, rq), 0.0, MASKED_SCORE)

    rj = lax.broadcasted_iota(jnp.int32, (tk, tk), 0)
    cs = lax.broadcasted_iota(jnp.int32, (tk, tk), 1)
    suffix = jnp.where(rj >= cs, 1.0, 0.0).astype(BF16)

    def visit(v):
        if v < 2:
            slot, tile = v, blocks[v]
        else:
            second = v - 2 >= i
            slot = second.astype(jnp.int32)
            tile = jnp.where(second, nq - v, i + 1 - v)
        return slot, pl.multiple_of(tile * tk, tk)

    def scores(v):
        slot, start = visit(v)
        z = _dot_nt(qst_ref[slot], k_ref[pl.ds(start, tk), :])
        return z + diag_bias if v < 2 else z

    def weights(v, z):
        slot, _ = visit(v)
        drop = jnp.maximum(z, 0.0) + jnp.log(1.0 + jnp.exp2(jnp.abs(z) * -LOG2E))
        parked = jnp.where(last_lane, pltpu.roll(total_ref[slot], LANES - 1, 1), 0.0)
        drop = jnp.concatenate([drop[:, 0:tk - LANES], drop[:, tk - LANES:tk] + parked], axis=1)
        inc = _dot(drop.astype(BF16), suffix)
        total_ref[slot] = inc[:, 0:LANES]
        return jnp.exp2((z - inc) * LOG2E).astype(BF16)

    def output(v, w):
        slot, start = visit(v)
        acc_ref[slot] += _dot(w, v_ref[pl.ds(start, tk), :])

    n_visits = nq + 1
    z, w = {}, {}
    for n in range(n_visits + 2):
        if n < n_visits:
            z[n] = scores(n)
        if 1 <= n <= n_visits:
            w[n - 1] = weights(n - 1, z.pop(n - 1))
        if n >= 2:
            output(n - 2, w.pop(n - 2))

    for s in range(2):
        acc = acc_ref[s]
        o2 = jnp.where(first, acc[0:tq], acc[tq:2 * tq])
        sq = o2 * o2
        s_first = jnp.sum(jnp.where(first, sq, 0.0), axis=-1, keepdims=True)
        s_all = jnp.sum(sq, axis=-1, keepdims=True)
        ms = jnp.where(first, s_first, s_all - s_first) * (1.0 / SB_HEAD_DIM)
        o_ref[pl.ds(pl.multiple_of(blocks[s] * tq, tq), tq), :] = (
            o2 * lax.rsqrt(ms + EPS) * nw_ref[...]).astype(BF16)


def _sb_attention(proj, sb_norm_w2, batch, seq, col0, tq=256):
    n = proj.shape[0]
    nq = seq // tq
    assert nq % 2 == 0
    seq_block = lambda c: pl.BlockSpec((seq, LANES), lambda b, p, i: (b, c + p))
    return pl.pallas_call(
        functools.partial(_sb_kernel, tq=tq, nq=nq),
        grid=(batch, SB_PAIRS, nq // 2),
        in_specs=[seq_block(col0), seq_block(col0 + SB_PAIRS), seq_block(col0 + 2 * SB_PAIRS),
                  pl.BlockSpec((1, LANES), lambda b, p, i: (0, 0))],
        out_specs=pl.BlockSpec((seq, LANES), lambda b, p, i: (b, p)),
        out_shape=jax.ShapeDtypeStruct((n, SB_WIDTH), BF16),
        scratch_shapes=[pltpu.VMEM((2, 2 * tq, LANES), BF16),
                        pltpu.VMEM((2, 2 * tq, LANES), F32),
                        pltpu.VMEM((2, 2 * tq, LANES), F32)],
        compiler_params=_params("parallel", "parallel", "arbitrary"),
        name="sb_attention",
    )(proj, proj, proj, sb_norm_w2)


def _out_proj_kernel(dn_ref, sb_ref, x_ref, w_ref, nw_ref, o_ref):
    half = dn_ref.shape[1]
    mix = _dot(dn_ref[...], w_ref[0:half, :]) + _dot(sb_ref[...], w_ref[half:2 * half, :])
    o_ref[...] = x_ref[...] + _rms_rows(mix, nw_ref[...])


def _out_proj(dn, sb, x2, w_out, nw, tm=512):
    n, d = x2.shape
    half = dn.shape[1]
    return pl.pallas_call(
        _out_proj_kernel,
        grid=(n // tm,),
        in_specs=[pl.BlockSpec((tm, half), lambda i: (i, 0)),
                  pl.BlockSpec((tm, half), lambda i: (i, 0)),
                  pl.BlockSpec((tm, d), lambda i: (i, 0)),
                  pl.BlockSpec((2 * half, d), lambda i: (0, 0)),
                  pl.BlockSpec((1, d), lambda i: (0, 0))],
        out_specs=pl.BlockSpec((tm, d), lambda i: (i, 0)),
        out_shape=jax.ShapeDtypeStruct((n, d), F32),
        compiler_params=_params("parallel"),
        name="out_proj",
    )(dn, sb, x2, w_out, nw)


def _swiglu_partial(h, wg, wu, wd, tf):
    out = None
    for c0 in range(0, tf, FF_SUBCHUNK):
        cols = slice(c0, min(c0 + FF_SUBCHUNK, tf))
        g = _dot(h, wg(cols))
        u = _dot(h, wu(cols))
        part = _dot((g * _sigmoid(g) * u).astype(BF16), wd(cols))
        out = part if out is None else out + part
    return out


def _dense_ffn_kernel(x_ref, pre_ref, post_ref, wg_ref, wu_ref, wd_ref, o_ref, h_ref, acc_ref):
    j = pl.program_id(1)

    @pl.when(j == 0)
    def _():
        h_ref[...] = _rms_rows(x_ref[...], pre_ref[...]).astype(BF16)

    d = _swiglu_partial(h_ref[...], lambda c: wg_ref[:, c], lambda c: wu_ref[:, c],
                        lambda c: wd_ref[c, :], wg_ref.shape[1])

    @pl.when(j == 0)
    def _():
        acc_ref[...] = d

    @pl.when(j > 0)
    def _():
        acc_ref[...] += d

    @pl.when(j == pl.num_programs(1) - 1)
    def _():
        o_ref[...] = x_ref[...] + _rms_rows(acc_ref[...], post_ref[...])


def _dense_ffn(x2, pre, post, wg, wu, wd, tm=512, tf=2816):
    n, d = x2.shape
    ff = wg.shape[1]
    return pl.pallas_call(
        _dense_ffn_kernel,
        grid=(n // tm, ff // tf),
        in_specs=[pl.BlockSpec((tm, d), lambda i, j: (i, 0)),
                  pl.BlockSpec((1, d), lambda i, j: (0, 0)),
                  pl.BlockSpec((1, d), lambda i, j: (0, 0)),
                  pl.BlockSpec((d, tf), lambda i, j: (0, j)),
                  pl.BlockSpec((d, tf), lambda i, j: (0, j)),
                  pl.BlockSpec((tf, d), lambda i, j: (j, 0))],
        out_specs=pl.BlockSpec((tm, d), lambda i, j: (i, 0)),
        out_shape=jax.ShapeDtypeStruct((n, d), F32),
        scratch_shapes=[pltpu.VMEM((tm, d), BF16), pltpu.VMEM((tm, d), F32)],
        compiler_params=_params("parallel", "arbitrary"),
        name="dense_ffn",
    )(x2, pre, post, wg, wu, wd)


def _router_kernel(x_ref, nw_ref, rhi_ref, rlo_ref, h_ref, meta_ref, cnt_ref, run_ref):
    tm = x_ref.shape[0]

    @pl.when(pl.program_id(0) == 0)
    def _():
        run_ref[...] = jnp.zeros_like(run_ref)

    hf = _rms_rows(x_ref[...], nw_ref[...])
    h_ref[...] = hf
    hb = hf.astype(BF16)
    hlo = (hf - hb.astype(F32)).astype(BF16)
    logits = _dot(hb, rhi_ref[...]) + _dot(hb, rlo_ref[...]) + _dot(hlo, rhi_ref[...])

    lane = lax.broadcasted_iota(jnp.int32, (tm, LANES), 1).astype(F32)
    lg = jnp.where(lane < N_EXPERTS, logits, -jnp.inf)
    m1 = jnp.max(lg, axis=-1, keepdims=True)
    i1 = jnp.min(jnp.where(lg == m1, lane, float(LANES)), axis=-1, keepdims=True)
    lg2 = jnp.where(lane == i1, -jnp.inf, lg)
    m2 = jnp.max(lg2, axis=-1, keepdims=True)
    i2 = jnp.min(jnp.where(lg2 == m2, lane, float(LANES)), axis=-1, keepdims=True)
    e = jnp.exp(m2 - m1)
    g1 = 1.0 / (1.0 + e)
    g2 = e * g1

    onehot = jnp.where(lane == i1, 1.0, jnp.where(lane == i2, 1.0, 0.0))
    r = lax.broadcasted_iota(jnp.int32, (tm, tm), 0)
    c = lax.broadcasted_iota(jnp.int32, (tm, tm), 1)
    before = jnp.where(c < r, 1.0, 0.0).astype(BF16)
    rank = _dot(before, onehot.astype(BF16)) + run_ref[...]
    r1 = jnp.sum(jnp.where(lane == i1, rank, 0.0), axis=-1, keepdims=True)
    r2 = jnp.sum(jnp.where(lane == i2, rank, 0.0), axis=-1, keepdims=True)
    run_ref[...] += jnp.sum(onehot, axis=0, keepdims=True)
    cnt_ref[...] = run_ref[...]

    meta = jnp.zeros((tm, LANES), F32)
    for k, val in enumerate((i1, i2, g1, g2, r1, r2)):
        meta = jnp.where(lane == float(k), val, meta)
    meta_ref[...] = meta


def _router(x2, nw, r_hi, r_lo, tm=512):
    n, d = x2.shape
    return pl.pallas_call(
        _router_kernel,
        grid=(n // tm,),
        in_specs=[pl.BlockSpec((tm, d), lambda i: (i, 0)),
                  pl.BlockSpec((1, d), lambda i: (0, 0)),
                  pl.BlockSpec((d, LANES), lambda i: (0, 0)),
                  pl.BlockSpec((d, LANES), lambda i: (0, 0))],
        out_specs=[pl.BlockSpec((tm, d), lambda i: (i, 0)),
                   pl.BlockSpec((tm, LANES), lambda i: (i, 0)),
                   pl.BlockSpec((1, LANES), lambda i: (0, 0))],
        out_shape=[jax.ShapeDtypeStruct((n, d), F32),
                   jax.ShapeDtypeStruct((n, LANES), F32),
                   jax.ShapeDtypeStruct((1, LANES), F32)],
        scratch_shapes=[pltpu.VMEM((1, LANES), F32)],
        compiler_params=_params("arbitrary"),
        name="router",
    )(x2, nw, r_hi, r_lo)


def _row_copy(src_ref, src_row, dst_ref, dst_row, sem):
    return pltpu.make_async_copy(src_ref.at[pl.ds(src_row, 1)], dst_ref.at[pl.ds(dst_row, 1)], sem)


def _dispatch_kernel(pos_ref, ends_ref, h_ref, xs_ref, zero_ref, sem, zero_sem, *, tb, tm):
    @pl.when(pl.program_id(0) == 0)
    def _():
        zero_ref[...] = jnp.zeros_like(zero_ref)

        def fill(e):
            if e < N_EXPERTS:
                start = ends_ref[e] - tm
                wanted = start >= (ends_ref[e - 1] if e else 0)
            else:
                start = ends_ref[N_EXPERTS - 1] + (e - N_EXPERTS) * tm
                wanted = start < xs_ref.shape[0]
            start = pl.multiple_of(jnp.clip(start, 0, xs_ref.shape[0] - tm), tm)
            return wanted, pltpu.make_async_copy(zero_ref, xs_ref.at[pl.ds(start, tm)], zero_sem)

        for e in range(2 * N_EXPERTS):
            wanted, cp = fill(e)
            pl.when(wanted)(cp.start)
        for e in range(2 * N_EXPERTS):
            wanted, cp = fill(e)
            pl.when(wanted)(cp.wait)

    def copies(t):
        return (_row_copy(h_ref, t, xs_ref, pos_ref[2 * t], sem),
                _row_copy(h_ref, t, xs_ref, pos_ref[2 * t + 1], sem))

    def issue(g, carry):
        for u in range(ROW_DMA_UNROLL):
            for k, cp in enumerate(copies(g * ROW_DMA_UNROLL + u)):
                cp.start(priority=k)
        return carry

    lax.fori_loop(0, tb // ROW_DMA_UNROLL, issue, 0)

    def drain(g, carry):
        for u in range(ROW_DMA_UNROLL):
            for cp in copies(g * ROW_DMA_UNROLL + u):
                cp.wait()
        return carry

    lax.fori_loop(0, tb // ROW_DMA_UNROLL, drain, 0)


def _dispatch(pos_flat, ends, h, rows, tm, tb=512):
    n, d = h.shape
    return pl.pallas_call(
        functools.partial(_dispatch_kernel, tb=tb, tm=tm),
        grid=(n // tb,),
        in_specs=[pl.BlockSpec((2 * tb,), lambda i: (i,), memory_space=pltpu.SMEM),
                  pl.BlockSpec(memory_space=pltpu.SMEM),
                  pl.BlockSpec((tb, d), lambda i: (i, 0))],
        out_specs=pl.BlockSpec(memory_space=pl.ANY),
        out_shape=jax.ShapeDtypeStruct((rows, d), h.dtype),
        scratch_shapes=[pltpu.VMEM((tm, d), h.dtype), pltpu.SemaphoreType.DMA(()),
                        pltpu.SemaphoreType.DMA(())],
        compiler_params=_params("arbitrary"),
        name="moe_dispatch",
    )(pos_flat, ends, h)


def _moe_ffn_kernel(te_ref, nv_ref, xs_ref, wg_ref, wu_ref, wd_ref, ys_ref, h_ref, acc_ref):
    del te_ref
    i = pl.program_id(0)
    j = pl.program_id(1)
    last = pl.num_programs(1) - 1
    valid = i < nv_ref[0]

    @pl.when(valid)
    def _():
        @pl.when(j == 0)
        def _():
            h_ref[...] = xs_ref[...].astype(BF16)

        d = _swiglu_partial(h_ref[...], lambda c: wg_ref[0, :, c], lambda c: wu_ref[0, :, c],
                            lambda c: wd_ref[0, c, :], wg_ref.shape[2])

        @pl.when(j == 0)
        def _():
            acc_ref[...] = d

        @pl.when(j > 0)
        def _():
            acc_ref[...] += d

        @pl.when(j == last)
        def _():
            ys_ref[...] = acc_ref[...]

    @pl.when(jnp.logical_and(jnp.logical_not(valid), j == last))
    def _():
        ys_ref[...] = jnp.zeros_like(ys_ref)


def _moe_ffn(tile_expert, n_valid, xs, wg, wu, wd, tm, tf):
    rows, d = xs.shape
    ff = wg.shape[2]
    grid_spec = pltpu.PrefetchScalarGridSpec(
        num_scalar_prefetch=2,
        grid=(rows // tm, ff // tf),
        in_specs=[pl.BlockSpec((tm, d), lambda i, j, te, nv: (jnp.minimum(i, nv[0] - 1), 0)),
                  pl.BlockSpec((1, d, tf), lambda i, j, te, nv: (te[i], 0, j)),
                  pl.BlockSpec((1, d, tf), lambda i, j, te, nv: (te[i], 0, j)),
                  pl.BlockSpec((1, tf, d), lambda i, j, te, nv: (te[i], j, 0))],
        out_specs=pl.BlockSpec((tm, d), lambda i, j, te, nv: (i, 0)),
        scratch_shapes=[pltpu.VMEM((tm, d), BF16), pltpu.VMEM((tm, d), F32)],
    )
    return pl.pallas_call(
        _moe_ffn_kernel,
        grid_spec=grid_spec,
        out_shape=jax.ShapeDtypeStruct((rows, d), F32),
        compiler_params=_params("arbitrary", "arbitrary"),
        name="moe_ffn",
    )(tile_expert, n_valid, xs, wg, wu, wd)


def _combine_kernel(pos_ref, pos_next_ref, ys_ref, x_ref, meta_ref, nw_ref, o_ref, y0_ref, y1_ref,
                    sem, *, tb):
    i = pl.program_id(0)
    slot = i % 2

    def copies(p_ref, s, t):
        return (_row_copy(ys_ref, p_ref[2 * t], y0_ref.at[s], t, sem.at[s]),
                _row_copy(ys_ref, p_ref[2 * t + 1], y1_ref.at[s], t, sem.at[s]))

    def issue(p_ref, s):
        def body(g, carry):
            for u in range(ROW_DMA_UNROLL):
                for cp in copies(p_ref, s, g * ROW_DMA_UNROLL + u):
                    cp.start()
            return carry
        lax.fori_loop(0, tb // ROW_DMA_UNROLL, body, 0)

    @pl.when(i == 0)
    def _():
        issue(pos_ref, 0)

    @pl.when(i + 1 < pl.num_programs(0))
    def _():
        issue(pos_next_ref, 1 - slot)

    def drain(g, carry):
        for u in range(ROW_DMA_UNROLL):
            for cp in copies(pos_ref, slot, g * ROW_DMA_UNROLL + u):
                cp.wait()
        return carry

    lax.fori_loop(0, tb // ROW_DMA_UNROLL, drain, 0)

    meta = meta_ref[...]
    y = meta[:, 2:3] * y0_ref[slot] + meta[:, 3:4] * y1_ref[slot]
    o_ref[...] = x_ref[...] + _rms_rows(y, nw_ref[...])


def _combine(pos_flat, ys, x2, meta, nw, tb=512):
    n, d = x2.shape
    steps = n // tb
    return pl.pallas_call(
        functools.partial(_combine_kernel, tb=tb),
        grid=(steps,),
        in_specs=[pl.BlockSpec((2 * tb,), lambda i: (i,), memory_space=pltpu.SMEM),
                  pl.BlockSpec((2 * tb,), lambda i: (jnp.minimum(i + 1, steps - 1),),
                               memory_space=pltpu.SMEM),
                  pl.BlockSpec(memory_space=pl.ANY),
                  pl.BlockSpec((tb, d), lambda i: (i, 0)),
                  pl.BlockSpec((tb, LANES), lambda i: (i, 0)),
                  pl.BlockSpec((1, d), lambda i: (0, 0))],
        out_specs=pl.BlockSpec((tb, d), lambda i: (i, 0)),
        out_shape=jax.ShapeDtypeStruct((n, d), F32),
        scratch_shapes=[pltpu.VMEM((2, tb, d), F32), pltpu.VMEM((2, tb, d), F32),
                        pltpu.SemaphoreType.DMA((2,))],
        compiler_params=_params("arbitrary"),
        name="moe_combine",
    )(pos_flat, pos_flat, ys, x2, meta, nw)


def _moe_layer(x2, pre, post, router_w, wg, wu, wd, tm=512, tf=3584):
    n, d = x2.shape
    r_pad = jnp.pad(router_w, ((0, 0), (0, LANES - N_EXPERTS)))
    r_hi = r_pad.astype(BF16)
    r_lo = (r_pad - r_hi.astype(F32)).astype(BF16)
    h, meta, cnt = _router(x2, pre, r_hi, r_lo)

    idx = meta[:, 0:2].astype(jnp.int32)
    rank = meta[:, 4:6].astype(jnp.int32)
    counts = cnt[0, 0:N_EXPERTS].astype(jnp.int32)
    padded = ((counts + tm - 1) // tm) * tm
    ends = jnp.cumsum(padded)
    offsets = ends - padded
    pos_flat = (offsets[idx] + rank).reshape(-1)
    n_tiles = (2 * n) // tm + N_EXPERTS
    n_valid = (ends[-1:] // tm).astype(jnp.int32)
    tile_start = jnp.minimum(jnp.arange(n_tiles, dtype=jnp.int32), n_valid - 1) * tm
    tile_expert = jnp.sum((tile_start[:, None] >= ends[None, :]).astype(jnp.int32), axis=1)

    xs = _dispatch(pos_flat, ends.astype(jnp.int32), h, n_tiles * tm, tm)
    ys = _moe_ffn(tile_expert, n_valid, xs, wg.astype(BF16), wu.astype(BF16), wd.astype(BF16),
                  tm, tf)
    return _combine(pos_flat, ys, x2, meta, post)


def kernel(x, norm_mix_pre, norm_mix_post, norm_ffn_pre, norm_ffn_post, w_in, conv_w, dn_a_log,
           dn_dt_bias, dn_norm_w, sb_norm_w, w_out, ffn_w_gate, ffn_w_up, ffn_w_down, router_w,
           moe_w_gate, moe_w_up, moe_w_down):
    batch, seq, d = x.shape
    depth = w_in.shape[0]
    x2 = x.reshape(batch * seq, d)
    gate_cols = 4 * DN_WIDTH
    n_small = 2 * DN_HEADS
    sb_col0 = gate_cols // LANES

    def pad_lanes(v, lane0):
        return jnp.zeros((1, LANES), F32).at[0, lane0:lane0 + v.shape[0]].set(v)

    for layer in range(depth):
        w = w_in[layer]
        w_big = jnp.concatenate([w[:, :gate_cols], w[:, gate_cols + n_small:]], axis=1).astype(BF16)
        w_small = jnp.pad(w[:, gate_cols:gate_cols + n_small],
                          ((0, 0), (0, LANES - n_small))).astype(BF16)
        proj, small = _in_proj(x2, norm_mix_pre[layer][None, :], w_big, w_small, conv_w[layer],
                               seq)
        dn = _deltanet(proj, small, pad_lanes(dn_a_log[layer], DN_HEADS),
                       pad_lanes(dn_dt_bias[layer], DN_HEADS), dn_norm_w[layer][None, :],
                       batch, seq)
        sb = _sb_attention(proj, jnp.tile(sb_norm_w[layer], 2)[None, :], batch, seq, sb_col0)
        x2 = _out_proj(dn, sb, x2, w_out[layer].astype(BF16), norm_mix_post[layer][None, :])

        pre = norm_ffn_pre[layer][None, :]
        post = norm_ffn_post[layer][None, :]
        i = layer // 2
        if layer % 2 == 0:
            x2 = _dense_ffn(x2, pre, post, ffn_w_gate[i].astype(BF16), ffn_w_up[i].astype(BF16),
                            ffn_w_down[i].astype(BF16))
        else:
            x2 = _moe_layer(x2, pre, post, router_w[i], moe_w_gate[i], moe_w_up[i], moe_w_down[i])
    return x2.reshape(batch, seq, d)
```
